```python
import jax, jax.numpy as jnp
from jax import lax
import numpy as np

D_MODEL = 1024
BATCH = 8
SEQ = 8192
DEPTH = 1

EPS = 1e-6
D_FF = 2816
A_WIDTH = 1024
A_GROUPS = 8
A_GROUP_DIM = A_WIDTH // A_GROUPS
A_CHUNK = 128
B_HEADS = 4
B_KEY_DIM = D_MODEL // 2
B_VAL_DIM = D_MODEL
B_HEAD_K = B_KEY_DIM // B_HEADS
B_HEAD_V = B_VAL_DIM // B_HEADS
B_GATE_RANK = 16
B_GATE_TAU = 16.0
B_CHUNK = 64
IN_SIZES = (A_WIDTH, A_WIDTH,
            B_KEY_DIM, B_KEY_DIM,
            B_VAL_DIM, B_VAL_DIM,
            B_GATE_RANK,
            2 * D_MODEL)
D_IN = sum(IN_SIZES)

kernel_name = "hybrid_gmlp_gla_macaron_block"


def _rmsnorm(x, g):
    xf = x.astype(jnp.float32)
    y = xf * lax.rsqrt(jnp.mean(xf * xf, axis=-1, keepdims=True) + EPS)
    return (y * g.astype(jnp.float32)).astype(x.dtype)


def _layernorm(x, g, b):
    xf = x.astype(jnp.float32)
    mu = jnp.mean(xf, axis=-1, keepdims=True)
    var = jnp.mean(jnp.square(xf - mu), axis=-1, keepdims=True)
    y = (xf - mu) * lax.rsqrt(var + EPS)
    return (y * g.astype(jnp.float32) + b.astype(jnp.float32)).astype(x.dtype)


def _swiglu(x, w_in, w_out):
    a, b = jnp.split(x @ w_in, 2, axis=-1)
    return (jax.nn.silu(a) * b) @ w_out


def _split(z, sizes):
    idx = [int(i) for i in np.cumsum(sizes)[:-1]]
    return jnp.split(z, idx, axis=-1)


def _gmlp_mixer(u, v, ln_g, ln_b, w_s, b_s):
    bsz, s, _ = u.shape
    u = jax.nn.gelu(u, approximate=False)
    v = _layernorm(jax.nn.gelu(v, approximate=False), ln_g, ln_b)
    v = v.reshape(bsz, s // A_CHUNK, A_CHUNK, A_GROUPS, A_GROUP_DIM)
    mask = jnp.tril(jnp.ones((A_CHUNK, A_CHUNK), dtype=bool))
    ws = jnp.where(mask[None], w_s, jnp.zeros_like(w_s))
    sp = jnp.einsum('gij,bnjgc->bnigc', ws, v) + b_s.T[None, None, :, :, None]
    return u * sp.reshape(bsz, s, A_WIDTH)


def _gla_mixer(q, k, v, r, a_lr, w_alpha, b_alpha, head_g):
    bsz, s, _ = q.shape
    n = s // B_CHUNK
    f32 = jnp.float32
    log_a = jax.nn.log_sigmoid((a_lr @ w_alpha + b_alpha).astype(f32)) / B_GATE_TAU

    def heads(t, d):
        return t.astype(f32).reshape(bsz, n, B_CHUNK, B_HEADS, d).transpose(0, 3, 1, 2, 4)

    qh = heads(q, B_HEAD_K) * (B_HEAD_K ** -0.5)
    kh = heads(k, B_HEAD_K)
    vh = heads(v, B_HEAD_V)
    bcum = jnp.cumsum(heads(log_a, B_HEAD_K), axis=3)
    b_mid = bcum[:, :, :, B_CHUNK // 2 - 1:B_CHUNK // 2, :]
    b_last = bcum[:, :, :, -1, :]

    q_in = qh * jnp.exp(bcum - b_mid)
    k_in = kh * jnp.exp(b_mid - bcum)
    scores = jnp.einsum('bhnik,bhnjk->bhnij', q_in, k_in)
    cmask = jnp.tril(jnp.ones((B_CHUNK, B_CHUNK), dtype=bool))
    scores = jnp.where(cmask, scores, 0.0)
    o_intra = jnp.einsum('bhnij,bhnjv->bhniv', scores, vh)

    q_out = qh * jnp.exp(bcum)
    k_st = kh * jnp.exp(b_last[:, :, :, None, :] - bcum)
    decay = jnp.exp(b_last)

    def step(state, xs):
        qn, kn, vn, dn = xs
        o = jnp.einsum('bhik,bhkv->bhiv', qn, state)
        state = dn[..., None] * state + jnp.einsum('bhjk,bhjv->bhkv', kn, vn)
        return state, o

    s0 = jnp.zeros((bsz, B_HEADS, B_HEAD_K, B_HEAD_V), f32)
    xs = (jnp.moveaxis(q_out, 2, 0), jnp.moveaxis(k_st, 2, 0),
          jnp.moveaxis(vh, 2, 0), jnp.moveaxis(decay, 2, 0))
    _, o_inter = lax.scan(step, s0, xs)
    o = o_intra + jnp.moveaxis(o_inter, 0, 2)
    o = o.transpose(0, 2, 3, 1, 4).reshape(bsz, s, B_HEADS, B_HEAD_V)
    o = o * lax.rsqrt(jnp.mean(o * o, axis=-1, keepdims=True) + EPS) * head_g.astype(f32)
    o = o.reshape(bsz, s, B_VAL_DIM).astype(v.dtype)
    return o * jax.nn.silu(r)


def setup_inputs(seed: int = 0) -> dict:
    key = jax.random.key(seed)
    ks = iter(jax.random.split(key, 32))
    L = DEPTH

    def w(shape, fan_in):
        return jax.random.normal(next(ks), shape, jnp.float32) * fan_in ** -0.5

    def gain(shape):
        return 1.0 + 0.1 * jax.random.normal(next(ks), shape, jnp.float32)

    def small(shape, scale):
        return scale * jax.random.normal(next(ks), shape, jnp.float32)

    return {
        "x": jax.random.normal(next(ks), (BATCH, SEQ, D_MODEL), jnp.float32),
        "ffn1_norm": gain((L, D_MODEL)),
        "ffn1_w_in": w((L, D_MODEL, 2 * D_FF), D_MODEL),
        "ffn1_w_out": w((L, D_FF, D_MODEL), D_FF),
        "mix_norm": gain((L, D_MODEL)),
        "w_in": w((L, D_MODEL, D_IN), D_MODEL),
        "b_gate": small((L, 2 * D_MODEL), 0.1),
        "a_ln_gain": gain((L, A_WIDTH)),
        "a_ln_bias": small((L, A_WIDTH), 0.02),
        "a_w_s": w((L, A_GROUPS, A_CHUNK, A_CHUNK), A_CHUNK),
        "a_b_s": gain((L, A_GROUPS, A_CHUNK)),
        "b_w_alpha": w((L, B_GATE_RANK, B_KEY_DIM), B_GATE_RANK),
        "b_b_alpha": small((L, B_KEY_DIM), 0.5),
        "b_head_norm": gain((L, B_HEADS, B_HEAD_V)),
        "w_proj_a": w((L, A_WIDTH, D_MODEL), A_WIDTH),
        "w_proj_b": w((L, B_VAL_DIM, D_MODEL), B_VAL_DIM),
        "w_o": w((L, D_MODEL, D_MODEL), D_MODEL),
        "ffn2_norm": gain((L, D_MODEL)),
        "ffn2_w_in": w((L, D_MODEL, 2 * D_FF), D_MODEL),
        "ffn2_w_out": w((L, D_FF, D_MODEL), D_FF),
        "final_norm": gain((D_MODEL,)),
    }


def reference(x, ffn1_norm, ffn1_w_in, ffn1_w_out, mix_norm, w_in, b_gate,
              a_ln_gain, a_ln_bias, a_w_s, a_b_s, b_w_alpha, b_b_alpha, b_head_norm,
              w_proj_a, w_proj_b, w_o, ffn2_norm, ffn2_w_in, ffn2_w_out, final_norm):
    for l in range(DEPTH):
        x = x + 0.5 * _swiglu(_rmsnorm(x, ffn1_norm[l]), ffn1_w_in[l], ffn1_w_out[l])

        h = _rmsnorm(x, mix_norm[l])
        z = h @ w_in[l]
        u_a, v_a, q_b, k_b, v_b, r_b, alr_b, g = _split(z, IN_SIZES)
        gate_a, gate_b = jnp.split(jax.nn.sigmoid(g + b_gate[l]), 2, axis=-1)

        y_a = _gmlp_mixer(u_a, v_a, a_ln_gain[l], a_ln_bias[l], a_w_s[l], a_b_s[l]) @ w_proj_a[l]
        y_b = _gla_mixer(q_b, k_b, v_b, r_b, alr_b, b_w_alpha[l], b_b_alpha[l],
                         b_head_norm[l]) @ w_proj_b[l]
        x = x + (gate_a * y_a + gate_b * y_b) @ w_o[l]

        x = x + 0.5 * _swiglu(_rmsnorm(x, ffn2_norm[l]), ffn2_w_in[l], ffn2_w_out[l])
    return _rmsnorm(x, final_norm)
```

```python
import functools

import jax
import jax.numpy as jnp
from jax import lax
from jax.experimental import pallas as pl
from jax.experimental.pallas import tpu as pltpu

F32 = jnp.float32
BF16 = jnp.bfloat16

EPS = 1e-6
A_GROUPS = 8
A_CHUNK = 128
B_HEADS = 4
B_HEAD_K = 128
B_HEAD_V = 256
B_GATE_RANK = 16
B_GATE_TAU = 16.0
B_CHUNK = 64
LANES = 128

FFN_TOKENS = 512
MIX_TOKENS = 512
GLA_TOKENS = 512
VMEM_LIMIT_BYTES = 56 * 1024 * 1024


def _resident(shape):
    zeros = (0,) * len(shape)
    return pl.BlockSpec(shape, lambda *_: zeros, pipeline_mode=pl.Buffered(1))


def _rms(x, gain):
    return x * lax.rsqrt(jnp.mean(x * x, axis=-1, keepdims=True) + EPS) * gain


def _dot(a, b):
    return jnp.dot(a, b, preferred_element_type=F32)


def _dot_nt(a, b):
    return lax.dot_general(a, b, (((1,), (1,)), ((), ())), preferred_element_type=F32)


def _dot_tn(a, b):
    return lax.dot_general(a, b, (((0,), (0,)), ((), ())), preferred_element_type=F32)


def _ffn_kernel(x_ref, g_ref, wa_ref, wb_ref, wo_ref, fg_ref, o_ref, *, final):
    x = x_ref[...]
    xn = _rms(x, g_ref[...]).astype(BF16)
    a = _dot(xn, wa_ref[...])
    b = _dot(xn, wb_ref[...])
    hidden = (a * jax.nn.sigmoid(a) * b).astype(BF16)
    y = x + 0.5 * _dot(hidden, wo_ref[...])
    if final:
        y = _rms(y, fg_ref[...])
    o_ref[...] = y


def _ffn(x, gain, w_in, w_out, final_gain, *, final):
    t, d = x.shape
    d_ff = w_out.shape[0]
    wa = w_in[:, :d_ff].astype(BF16)
    wb = w_in[:, d_ff:].astype(BF16)
    wo = w_out.astype(BF16)
    tok = pl.BlockSpec((FFN_TOKENS, d), lambda i: (i, 0))
    return pl.pallas_call(
        functools.partial(_ffn_kernel, final=final),
        grid=(t // FFN_TOKENS,),
        in_specs=[tok, _resident((1, d)), _resident((d, d_ff)), _resident((d, d_ff)),
                  _resident((d_ff, d)), _resident((1, d))],
        out_specs=tok,
        out_shape=jax.ShapeDtypeStruct((t, d), F32),
        compiler_params=pltpu.CompilerParams(
            dimension_semantics=("arbitrary",), vmem_limit_bytes=VMEM_LIMIT_BYTES),
        name="ffn_final" if final else "ffn",
    )(x, gain.reshape(1, d), wa, wb, wo, final_gain.reshape(1, d))


def _gelu(x):
    return 0.5 * x * (1.0 + lax.erf(x * (2.0 ** -0.5)))


def _mix_in_kernel(x_ref, g_ref, wu_ref, wv_ref, wqk_ref, wvb_ref, wr_ref, walr_ref,
                   wg_ref, bg_ref, lng_ref, lnb_ref, ws_ref, bs_ref, wpa_ref,
                   ma_ref, gb_ref, q_ref, k_ref, vb_ref, rs_ref, alr_ref):
    tm, d = x_ref.shape
    h = _rms(x_ref[...], g_ref[...]).astype(BF16)

    qk = _dot(h, wqk_ref[...])
    half = qk.shape[1] // 2
    q_ref[...] = qk[:, :half].astype(BF16)
    k_ref[...] = qk[:, half:].astype(BF16)
    vb_ref[...] = _dot(h, wvb_ref[...]).astype(BF16)
    r = _dot(h, wr_ref[...])
    rs_ref[...] = (r * jax.nn.sigmoid(r)).astype(BF16)
    alr_ref[...] = _dot(h, walr_ref[...]).astype(BF16)

    u = _gelu(_dot(h, wu_ref[...]))
    v = _gelu(_dot(h, wv_ref[...]))
    mu = jnp.mean(v, axis=-1, keepdims=True)
    vc = v - mu
    var = jnp.mean(vc * vc, axis=-1, keepdims=True)
    vn = (vc * lax.rsqrt(var + EPS) * lng_ref[...] + lnb_ref[...]).astype(BF16)
    width = vn.shape[1]
    gdim = width // A_GROUPS
    rows = []
    for n in range(tm // A_CHUNK):
        blocks = []
        for g in range(A_GROUPS):
            vblk = vn[n * A_CHUNK:(n + 1) * A_CHUNK, g * gdim:(g + 1) * gdim]
            blocks.append(_dot(ws_ref[g], vblk))
        rows.append(jnp.concatenate(blocks, axis=1) + bs_ref[...])
    sp = jnp.concatenate(rows, axis=0)
    ya = _dot((u * sp).astype(BF16), wpa_ref[...])

    gates = jax.nn.sigmoid(_dot(h, wg_ref[...]) + bg_ref[...])
    ma_ref[...] = (gates[:, :d] * ya).astype(BF16)
    gb_ref[...] = gates[:, d:].astype(BF16)


def _mix_in(x1, p):
    t, d = x1.shape
    tm = MIX_TOKENS

    def tok(width):
        return pl.BlockSpec((tm, width), lambda i: (i, 0))

    weights = [p["mix_norm"], p["wu"], p["wv"], p["wqk"], p["wvb"], p["wr"], p["walr"],
               p["wg"], p["b_gate"], p["ln_g"], p["ln_b"], p["ws"], p["bs"], p["wpa"]]
    kd = p["wqk"].shape[1] // 2
    vd = p["wvb"].shape[1]
    out_widths = [d, d, kd, kd, vd, vd, LANES]
    return pl.pallas_call(
        _mix_in_kernel,
        grid=(t // tm,),
        in_specs=[tok(d)] + [_resident(w.shape) for w in weights],
        out_specs=[tok(w) for w in out_widths],
        out_shape=[jax.ShapeDtypeStruct((t, w), BF16) for w in out_widths],
        compiler_params=pltpu.CompilerParams(
            dimension_semantics=("arbitrary",), vmem_limit_bytes=VMEM_LIMIT_BYTES),
        name="mix_in",
    )(x1, *weights)


def _gla_kernel(q_ref, k_ref, vb_ref, rs_ref, alr_ref, gb_ref, ma_ref, x1_ref,
                wal_ref, bal_ref, hg_ref, wpb_ref, wo_ref, o_ref,
                st_ref, qin_ref, kin_ref, qout_ref, kst_ref, dec_ref, oacc_ref):
    ts, kd = q_ref.shape
    nc = ts // B_CHUNK

    @pl.when(pl.program_id(1) == 0)
    def _():
        st_ref[...] = jnp.zeros_like(st_ref)

    pre = _dot(alr_ref[...], wal_ref[...]) + bal_ref[...]
    la = jax.nn.log_sigmoid(pre) / B_GATE_TAU
    pos = lax.broadcasted_iota(jnp.int32, la.shape, 0) % B_CHUNK
    bc = la
    shift = 1
    while shift < B_CHUNK:
        bc = bc + jnp.where(pos >= shift, pltpu.roll(bc, shift, 0), 0.0)
        shift *= 2
    bc = bc.reshape(nc, B_CHUNK, kd)
    b_mid = bc[:, B_CHUNK // 2 - 1:B_CHUNK // 2, :]
    b_last = bc[:, B_CHUNK - 1:B_CHUNK, :]

    q = q_ref[...].astype(F32).reshape(nc, B_CHUNK, kd) * (B_HEAD_K ** -0.5)
    k = k_ref[...].astype(F32).reshape(nc, B_CHUNK, kd)
    qin_ref[...] = (q * jnp.exp(bc - b_mid)).astype(BF16).reshape(ts, kd)
    kin_ref[...] = (k * jnp.exp(b_mid - bc)).astype(BF16).reshape(ts, kd)
    qout_ref[...] = (q * jnp.exp(bc)).astype(BF16).reshape(ts, kd)
    kst_ref[...] = (k * jnp.exp(b_last - bc)).astype(BF16).reshape(ts, kd)
    dec_ref[...] = jnp.broadcast_to(jnp.exp(b_last), dec_ref.shape)

    causal = (lax.broadcasted_iota(jnp.int32, (B_CHUNK, B_CHUNK), 0)
              >= lax.broadcasted_iota(jnp.int32, (B_CHUNK, B_CHUNK), 1))

    def chunk_step(c, carry):
        r0 = pl.multiple_of(c * B_CHUNK, B_CHUNK)
        rows = pl.ds(r0, B_CHUNK)
        dec = dec_ref[c]
        for hd in range(B_HEADS):
            kc = slice(hd * B_HEAD_K, (hd + 1) * B_HEAD_K)
            vc = slice(hd * B_HEAD_V, (hd + 1) * B_HEAD_V)
            v_c = vb_ref[rows, vc]
            scores = _dot_nt(qin_ref[rows, kc], kin_ref[rows, kc])
            scores = jnp.where(causal, scores, 0.0).astype(BF16)
            state_t = st_ref[hd]
            o = _dot(scores, v_c) + _dot_nt(qout_ref[rows, kc], state_t.astype(BF16))
            oacc_ref[rows, vc] = o
            st_ref[hd] = dec[0:1, kc] * state_t + _dot_tn(v_c, kst_ref[rows, kc])
        return carry

    lax.fori_loop(0, nc, chunk_step, 0)

    rs = rs_ref[...].astype(F32)
    parts = []
    for hd in range(B_HEADS):
        vc = slice(hd * B_HEAD_V, (hd + 1) * B_HEAD_V)
        parts.append(_rms(oacc_ref[:, vc], hg_ref[:, vc]) * rs[:, vc])
    yb = _dot(jnp.concatenate(parts, axis=1).astype(BF16), wpb_ref[...])
    merged = gb_ref[...].astype(F32) * yb + ma_ref[...].astype(F32)
    o_ref[...] = x1_ref[...] + _dot(merged.astype(BF16), wo_ref[...])


def _gla(q, k, vb, rs, alr, gb, ma, x1, p, batch, seq):
    t, d = x1.shape
    ts = GLA_TOKENS
    per_seq = seq // ts
    kd = q.shape[1]
    vd = vb.shape[1]

    def tok(width):
        return pl.BlockSpec((ts, width), lambda b, s: (b * per_seq + s, 0))

    weights = [p["wal"], p["b_alpha"], p["head_g"], p["wpb"], p["wo"]]
    return pl.pallas_call(
        _gla_kernel,
        grid=(batch, per_seq),
        in_specs=[tok(kd), tok(kd), tok(vd), tok(vd), tok(LANES), tok(d), tok(d), tok(d)]
        + [_resident(w.shape) for w in weights],
        out_specs=tok(d),
        out_shape=jax.ShapeDtypeStruct((t, d), F32),
        scratch_shapes=[
            pltpu.VMEM((B_HEADS, B_HEAD_V, B_HEAD_K), F32),
            pltpu.VMEM((ts, kd), BF16),
            pltpu.VMEM((ts, kd), BF16),
            pltpu.VMEM((ts, kd), BF16),
            pltpu.VMEM((ts, kd), BF16),
            pltpu.VMEM((ts // B_CHUNK, 8, kd), F32),
            pltpu.VMEM((ts, vd), F32),
        ],
        compiler_params=pltpu.CompilerParams(
            dimension_semantics=("arbitrary", "arbitrary"),
            vmem_limit_bytes=VMEM_LIMIT_BYTES),
        name="gla",
    )(q, k, vb, rs, alr, gb, ma, x1, *weights)


def kernel(x, ffn1_norm, ffn1_w_in, ffn1_w_out, mix_norm, w_in, b_gate, a_ln_gain, a_ln_bias, a_w_s, a_b_s, b_w_alpha, b_b_alpha, b_head_norm, w_proj_a, w_proj_b, w_o, ffn2_norm, ffn2_w_in, ffn2_w_out, final_norm):
    batch, seq, d = x.shape
    depth = ffn1_norm.shape[0]
    a_width = a_ln_gain.shape[1]
    kd = b_w_alpha.shape[2]
    vd = w_proj_b.shape[1]
    assert seq % max(FFN_TOKENS, MIX_TOKENS, GLA_TOKENS) == 0
    assert MIX_TOKENS % A_CHUNK == 0 and GLA_TOKENS % B_CHUNK == 0
    xt = x.reshape(batch * seq, d)

    tril = jnp.tril(jnp.ones((A_CHUNK, A_CHUNK), dtype=bool))
    for l in range(depth):
        xt = _ffn(xt, ffn1_norm[l], ffn1_w_in[l], ffn1_w_out[l], final_norm, final=False)

        w = w_in[l].astype(BF16)
        o_u, o_v, o_q = 0, a_width, 2 * a_width
        o_vb = o_q + 2 * kd
        o_r = o_vb + vd
        o_alr = o_r + vd
        o_g = o_alr + B_GATE_RANK
        pad = LANES - B_GATE_RANK
        p = {
            "mix_norm": mix_norm[l].reshape(1, d),
            "wu": w[:, o_u:o_v], "wv": w[:, o_v:o_q], "wqk": w[:, o_q:o_vb],
            "wvb": w[:, o_vb:o_r], "wr": w[:, o_r:o_alr],
            "walr": jnp.pad(w[:, o_alr:o_g], ((0, 0), (0, pad))),
            "wg": w[:, o_g:], "b_gate": b_gate[l].reshape(1, 2 * d),
            "ln_g": a_ln_gain[l].reshape(1, a_width), "ln_b": a_ln_bias[l].reshape(1, a_width),
            "ws": jnp.where(tril[None], a_w_s[l], 0.0).astype(BF16),
            "bs": jnp.repeat(a_b_s[l].T, a_width // A_GROUPS, axis=1),
            "wpa": w_proj_a[l].astype(BF16),
            "wal": jnp.pad(b_w_alpha[l].astype(BF16), ((0, pad), (0, 0))),
            "b_alpha": b_b_alpha[l].reshape(1, kd),
            "head_g": b_head_norm[l].reshape(1, vd),
            "wpb": w_proj_b[l].astype(BF16), "wo": w_o[l].astype(BF16),
        }
        ma, gb, q, k, vb, rs, alr = _mix_in(xt, p)
        xt = _gla(q, k, vb, rs, alr, gb, ma, xt, p, batch, seq)

        last = l == depth - 1
        xt = _ffn(xt, ffn2_norm[l], ffn2_w_in[l], ffn2_w_out[l], final_norm, final=last)
    return xt.reshape(batch, seq, d)
```

```python
import functools

import jax
import jax.numpy as jnp
from jax import lax
from jax.experimental import pallas as pl
from jax.experimental.pallas import tpu as pltpu

F32 = jnp.float32
BF16 = jnp.bfloat16

EPS = 1e-6
A_GROUPS = 8
A_CHUNK = 128
B_HEADS = 4
B_HEAD_K = 128
B_HEAD_V = 256
B_GATE_RANK = 16
B_GATE_TAU = 16.0
B_CHUNK = 64
LANES = 128

FFN_TOKENS = 512
MIX_TOKENS = 512
GLA_TOKENS = 512
VMEM_LIMIT_BYTES = 56 * 1024 * 1024


def _resident(shape):
    zeros = (0,) * len(shape)
    return pl.BlockSpec(shape, lambda *_: zeros, pipeline_mode=pl.Buffered(1))


def _rms(x, gain):
    return x * lax.rsqrt(jnp.mean(x * x, axis=-1, keepdims=True) + EPS) * gain


def _dot(a, b):
    return jnp.dot(a, b, preferred_element_type=F32)


def _dot_nt(a, b):
    return lax.dot_general(a, b, (((1,), (1,)), ((), ())), preferred_element_type=F32)


def _dot_tn(a, b):
    return lax.dot_general(a, b, (((0,), (0,)), ((), ())), preferred_element_type=F32)


def _ffn_kernel(x_ref, g_ref, wa_ref, wb_ref, wo_ref, fg_ref, o_ref, *, final):
    x = x_ref[...]
    xn = _rms(x, g_ref[...]).astype(BF16)
    a = _dot(xn, wa_ref[...])
    b = _dot(xn, wb_ref[...])
    hidden = (a * jax.nn.sigmoid(a) * b).astype(BF16)
    y = x + 0.5 * _dot(hidden, wo_ref[...])
    if final:
        y = _rms(y, fg_ref[...])
    o_ref[...] = y


def _ffn(x, gain, w_in, w_out, final_gain, *, final):
    t, d = x.shape
    d_ff = w_out.shape[0]
    wa = w_in[:, :d_ff].astype(BF16)
    wb = w_in[:, d_ff:].astype(BF16)
    wo = w_out.astype(BF16)
    tok = pl.BlockSpec((FFN_TOKENS, d), lambda i: (i, 0))
    return pl.pallas_call(
        functools.partial(_ffn_kernel, final=final),
        grid=(t // FFN_TOKENS,),
        in_specs=[tok, _resident((1, d)), _resident((d, d_ff)), _resident((d, d_ff)),
                  _resident((d_ff, d)), _resident((1, d))],
        out_specs=tok,
        out_shape=jax.ShapeDtypeStruct((t, d), F32),
        compiler_params=pltpu.CompilerParams(
            dimension_semantics=("arbitrary",), vmem_limit_bytes=VMEM_LIMIT_BYTES),
        name="ffn_final" if final else "ffn",
    )(x, gain.reshape(1, d), wa, wb, wo, final_gain.reshape(1, d))


def _gelu(x):
    return 0.5 * x * (1.0 + lax.erf(x * (2.0 ** -0.5)))


def _mix_in_kernel(x_ref, g_ref, wu_ref, wv_ref, wqk_ref, wvb_ref, wr_ref, walr_ref,
                   wal_ref, bal_ref, wg_ref, bg_ref, lng_ref, lnb_ref, ws_ref, bs_ref,
                   wpa_ref, ma_ref, gb_ref, qin_ref, kin_ref, qout_ref, kst_ref, dec_ref,
                   vb_ref, rs_ref, qk_ref, zu_ref, ga_ref):
    tm, d = x_ref.shape
    h = _rms(x_ref[...], g_ref[...]).astype(BF16)

    alr = _dot(h, walr_ref[...]).astype(BF16)
    zv = _dot(h, wv_ref[...])
    qk_ref[...] = _dot(h, wqk_ref[...])
    pre = _dot(alr, wal_ref[...]) + bal_ref[...]
    zu_ref[...] = _dot(h, wu_ref[...])
    vb_ref[...] = _dot(h, wvb_ref[...]).astype(BF16)
    r = _dot(h, wr_ref[...])
    rs_ref[...] = (r * jax.nn.sigmoid(r)).astype(BF16)
    gates = jax.nn.sigmoid(_dot(h, wg_ref[...]) + bg_ref[...])
    ga_ref[...] = gates[:, :d]
    gb_ref[...] = gates[:, d:].astype(BF16)

    la = jax.nn.log_sigmoid(pre) / B_GATE_TAU
    kd = la.shape[1]
    nc = tm // B_CHUNK
    pos = lax.broadcasted_iota(jnp.int32, la.shape, 0) % B_CHUNK
    bc = la
    shift = 1
    while shift < B_CHUNK:
        bc = bc + jnp.where(pos >= shift, pltpu.roll(bc, shift, 0), 0.0)
        shift *= 2
    bc = bc.reshape(nc, B_CHUNK, kd)
    b_mid = bc[:, B_CHUNK // 2 - 1:B_CHUNK // 2, :]
    b_last = bc[:, B_CHUNK - 1:B_CHUNK, :]
    dec_ref[...] = jnp.exp(b_last).reshape(nc, kd)

    q = qk_ref[:, :kd].reshape(nc, B_CHUNK, kd) * (B_HEAD_K ** -0.5)
    k = qk_ref[:, kd:].reshape(nc, B_CHUNK, kd)
    qin_ref[...] = (q * jnp.exp(bc - b_mid)).astype(BF16).reshape(tm, kd)
    kin_ref[...] = (k * jnp.exp(b_mid - bc)).astype(BF16).reshape(tm, kd)
    qout_ref[...] = (q * jnp.exp(bc)).astype(BF16).reshape(tm, kd)
    kst_ref[...] = (k * jnp.exp(b_last - bc)).astype(BF16).reshape(tm, kd)

    v = _gelu(zv)
    mu = jnp.mean(v, axis=-1, keepdims=True)
    vc = v - mu
    var = jnp.mean(vc * vc, axis=-1, keepdims=True)
    vn = (vc * lax.rsqrt(var + EPS) * lng_ref[...] + lnb_ref[...]).astype(BF16)
    width = vn.shape[1]
    gdim = width // A_GROUPS
    rows = []
    for n in range(tm // A_CHUNK):
        blocks = []
        for g in range(A_GROUPS):
            vblk = vn[n * A_CHUNK:(n + 1) * A_CHUNK, g * gdim:(g + 1) * gdim]
            blocks.append(_dot(ws_ref[g], vblk))
        rows.append(jnp.concatenate(blocks, axis=1) + bs_ref[...])
    sp = jnp.concatenate(rows, axis=0)
    ya = _dot((_gelu(zu_ref[...]) * sp).astype(BF16), wpa_ref[...])
    ma_ref[...] = (ga_ref[...] * ya).astype(BF16)


def _mix_in(x1, p):
    t, d = x1.shape
    tm = MIX_TOKENS

    def tok(width):
        return pl.BlockSpec((tm, width), lambda i: (i, 0))

    weights = [p["mix_norm"], p["wu"], p["wv"], p["wqk"], p["wvb"], p["wr"], p["walr"],
               p["wal"], p["b_alpha"], p["wg"], p["b_gate"], p["ln_g"], p["ln_b"], p["ws"],
               p["bs"], p["wpa"]]
    kd = p["wqk"].shape[1] // 2
    vd = p["wvb"].shape[1]
    out_widths = [d, d, kd, kd, kd, kd, vd, vd]
    out_specs = [tok(w) for w in out_widths]
    out_shape = [jax.ShapeDtypeStruct((t, w), BF16) for w in out_widths]
    out_specs.insert(6, pl.BlockSpec((tm // B_CHUNK, kd), lambda i: (i, 0)))
    out_shape.insert(6, jax.ShapeDtypeStruct((t // B_CHUNK, kd), F32))
    return pl.pallas_call(
        _mix_in_kernel,
        grid=(t // tm,),
        in_specs=[tok(d)] + [_resident(w.shape) for w in weights],
        out_specs=out_specs,
        out_shape=out_shape,
        scratch_shapes=[pltpu.VMEM((tm, 2 * kd), F32),
                        pltpu.VMEM((tm, d), F32),
                        pltpu.VMEM((tm, d), F32)],
        compiler_params=pltpu.CompilerParams(
            dimension_semantics=("arbitrary",), vmem_limit_bytes=VMEM_LIMIT_BYTES),
        name="mix_in",
    )(x1, *weights)


def _gla_kernel(qin_ref, kin_ref, qout_ref, kst_ref, dec_ref, vb_ref, rs_ref, gb_ref,
                ma_ref, x1_ref, hg_ref, wpb_ref, wo_ref, o_ref, st_ref, oacc_ref):
    ts, kd = qin_ref.shape
    nc = ts // B_CHUNK

    @pl.when(pl.program_id(1) == 0)
    def _():
        st_ref[...] = jnp.zeros_like(st_ref)

    causal = (lax.broadcasted_iota(jnp.int32, (B_CHUNK, B_CHUNK), 0)
              >= lax.broadcasted_iota(jnp.int32, (B_CHUNK, B_CHUNK), 1))

    pairs = [(c, hd) for c in range(nc) for hd in range(B_HEADS)]

    def rows(c):
        return slice(c * B_CHUNK, (c + 1) * B_CHUNK)

    def kcols(hd):
        return slice(hd * B_HEAD_K, (hd + 1) * B_HEAD_K)

    def vcols(hd):
        return slice(hd * B_HEAD_V, (hd + 1) * B_HEAD_V)

    scores, kv = {}, {}
    for c, hd in pairs:
        s = _dot_nt(qin_ref[rows(c), kcols(hd)], kin_ref[rows(c), kcols(hd)])
        scores[c, hd] = jnp.where(causal, s, 0.0).astype(BF16)
        kv[c, hd] = _dot_tn(vb_ref[rows(c), vcols(hd)], kst_ref[rows(c), kcols(hd)])
    for c, hd in pairs:
        oacc_ref[rows(c), vcols(hd)] = _dot(scores[c, hd], vb_ref[rows(c), vcols(hd)])
    state_t = [st_ref[hd] for hd in range(B_HEADS)]
    for c, hd in pairs:
        oacc_ref[rows(c), vcols(hd)] += _dot_nt(qout_ref[rows(c), kcols(hd)],
                                                state_t[hd].astype(BF16))
        state_t[hd] = dec_ref[c:c + 1, kcols(hd)] * state_t[hd] + kv[c, hd]
    for hd in range(B_HEADS):
        st_ref[hd] = state_t[hd]

    rs = rs_ref[...].astype(F32)
    parts = []
    for hd in range(B_HEADS):
        vc = slice(hd * B_HEAD_V, (hd + 1) * B_HEAD_V)
        parts.append(_rms(oacc_ref[:, vc], hg_ref[:, vc]) * rs[:, vc])
    yb = _dot(jnp.concatenate(parts, axis=1).astype(BF16), wpb_ref[...])
    merged = gb_ref[...].astype(F32) * yb + ma_ref[...].astype(F32)
    o_ref[...] = x1_ref[...] + _dot(merged.astype(BF16), wo_ref[...])


def _gla(qin, kin, qout, kst, dec, vb, rs, gb, ma, x1, p, batch, seq):
    t, d = x1.shape
    ts = GLA_TOKENS
    per_seq = seq // ts
    kd = qin.shape[1]
    vd = vb.shape[1]

    def tok(width, rows=ts):
        return pl.BlockSpec((rows, width), lambda b, s: (b * per_seq + s, 0))

    weights = [p["head_g"], p["wpb"], p["wo"]]
    return pl.pallas_call(
        _gla_kernel,
        grid=(batch, per_seq),
        in_specs=[tok(kd), tok(kd), tok(kd), tok(kd), tok(kd, ts // B_CHUNK), tok(vd),
                  tok(vd), tok(d), tok(d), tok(d)] + [_resident(w.shape) for w in weights],
        out_specs=tok(d),
        out_shape=jax.ShapeDtypeStruct((t, d), F32),
        scratch_shapes=[
            pltpu.VMEM((B_HEADS, B_HEAD_V, B_HEAD_K), F32),
            pltpu.VMEM((ts, vd), F32),
        ],
        compiler_params=pltpu.CompilerParams(
            dimension_semantics=("arbitrary", "arbitrary"),
            vmem_limit_bytes=VMEM_LIMIT_BYTES),
        name="gla",
    )(qin, kin, qout, kst, dec, vb, rs, gb, ma, x1, *weights)


def kernel(x, ffn1_norm, ffn1_w_in, ffn1_w_out, mix_norm, w_in, b_gate, a_ln_gain, a_ln_bias, a_w_s, a_b_s, b_w_alpha, b_b_alpha, b_head_norm, w_proj_a, w_proj_b, w_o, ffn2_norm, ffn2_w_in, ffn2_w_out, final_norm):
    batch, seq, d = x.shape
    depth = ffn1_norm.shape[0]
    a_width = a_ln_gain.shape[1]
    kd = b_w_alpha.shape[2]
    vd = w_proj_b.shape[1]
    assert seq % max(FFN_TOKENS, MIX_TOKENS, GLA_TOKENS) == 0
    assert MIX_TOKENS % A_CHUNK == 0 and GLA_TOKENS % B_CHUNK == 0
    xt = x.reshape(batch * seq, d)

    tril = jnp.tril(jnp.ones((A_CHUNK, A_CHUNK), dtype=bool))
    for l in range(depth):
        xt = _ffn(xt, ffn1_norm[l], ffn1_w_in[l], ffn1_w_out[l], final_norm, final=False)

        w = w_in[l].astype(BF16)
        o_u, o_v, o_q = 0, a_width, 2 * a_width
        o_vb = o_q + 2 * kd
        o_r = o_vb + vd
        o_alr = o_r + vd
        o_g = o_alr + B_GATE_RANK
        pad = LANES - B_GATE_RANK
        p = {
            "mix_norm": mix_norm[l].reshape(1, d),
            "wu": w[:, o_u:o_v], "wv": w[:, o_v:o_q], "wqk": w[:, o_q:o_vb],
            "wvb": w[:, o_vb:o_r], "wr": w[:, o_r:o_alr],
            "walr": jnp.pad(w[:, o_alr:o_g], ((0, 0), (0, pad))),
            "wg": w[:, o_g:], "b_gate": b_gate[l].reshape(1, 2 * d),
            "ln_g": a_ln_gain[l].reshape(1, a_width), "ln_b": a_ln_bias[l].reshape(1, a_width),
            "ws": jnp.where(tril[None], a_w_s[l], 0.0).astype(BF16),
            "bs": jnp.repeat(a_b_s[l].T, a_width // A_GROUPS, axis=1),
            "wpa": w_proj_a[l].astype(BF16),
            "wal": jnp.pad(b_w_alpha[l].astype(BF16), ((0, pad), (0, 0))),
            "b_alpha": b_b_alpha[l].reshape(1, kd),
            "head_g": b_head_norm[l].reshape(1, vd),
            "wpb": w_proj_b[l].astype(BF16), "wo": w_o[l].astype(BF16),
        }
        ma, gb, qin, kin, qout, kst, dec, vb, rs = _mix_in(xt, p)
        xt = _gla(qin, kin, qout, kst, dec, vb, rs, gb, ma, xt, p, batch, seq)

        last = l == depth - 1
        xt = _ffn(xt, ffn2_norm[l], ffn2_w_in[l], ffn2_w_out[l], final_norm, final=last)
    return xt.reshape(batch, seq, d)
```

```python
import functools

import jax
import jax.numpy as jnp
from jax import lax
from jax.experimental import pallas as pl
from jax.experimental.pallas import tpu as pltpu

F32 = jnp.float32
BF16 = jnp.bfloat16

EPS = 1e-6
A_GROUPS = 8
A_CHUNK = 128
B_HEADS = 4
B_HEAD_K = 128
B_HEAD_V = 256
B_GATE_RANK = 16
B_GATE_TAU = 16.0
B_CHUNK = 64
LANES = 128

FFN_TOKENS = 1024
FFN_SUBBLOCKS = 4
MIX_TOKENS = 512
MIX_SUBBLOCKS = 1
GLA_TOKENS = 512
VMEM_LIMIT_BYTES = 56 * 1024 * 1024


def _resident(shape):
    zeros = (0,) * len(shape)
    return pl.BlockSpec(shape, lambda *_: zeros, pipeline_mode=pl.Buffered(1))


def _rms(x, gain):
    return x * lax.rsqrt(jnp.mean(x * x, axis=-1, keepdims=True) + EPS) * gain


def _dot(a, b):
    return jnp.dot(a, b, preferred_element_type=F32)


def _dot_nt(a, b):
    return lax.dot_general(a, b, (((1,), (1,)), ((), ())), preferred_element_type=F32)


def _dot_tn(a, b):
    return lax.dot_general(a, b, (((0,), (0,)), ((), ())), preferred_element_type=F32)


def _ffn_kernel(x_ref, g_ref, wa_ref, wb_ref, wo_ref, fg_ref, o_ref, *, final):
    sub = x_ref.shape[0] // FFN_SUBBLOCKS
    for s in range(FFN_SUBBLOCKS):
        rows = slice(s * sub, (s + 1) * sub)
        x = x_ref[rows, :]
        xn = _rms(x, g_ref[...]).astype(BF16)
        a = _dot(xn, wa_ref[...])
        b = _dot(xn, wb_ref[...])
        hidden = (a * jax.nn.sigmoid(a) * b).astype(BF16)
        y = x + 0.5 * _dot(hidden, wo_ref[...])
        if final:
            y = _rms(y, fg_ref[...])
        o_ref[rows, :] = y


def _ffn(x, gain, w_in, w_out, final_gain, *, final):
    t, d = x.shape
    d_ff = w_out.shape[0]
    wa = w_in[:, :d_ff].astype(BF16)
    wb = w_in[:, d_ff:].astype(BF16)
    wo = w_out.astype(BF16)
    tok = pl.BlockSpec((FFN_TOKENS, d), lambda i: (i, 0))
    return pl.pallas_call(
        functools.partial(_ffn_kernel, final=final),
        grid=(t // FFN_TOKENS,),
        in_specs=[tok, _resident((1, d)), _resident((d, d_ff)), _resident((d, d_ff)),
                  _resident((d_ff, d)), _resident((1, d))],
        out_specs=tok,
        out_shape=jax.ShapeDtypeStruct((t, d), F32),
        compiler_params=pltpu.CompilerParams(
            dimension_semantics=("arbitrary",), vmem_limit_bytes=VMEM_LIMIT_BYTES),
        name="ffn_final" if final else "ffn",
    )(x, gain.reshape(1, d), wa, wb, wo, final_gain.reshape(1, d))


def _gelu(x):
    return 0.5 * x * (1.0 + lax.erf(x * (2.0 ** -0.5)))


def _mix_in_kernel(x_ref, g_ref, wu_ref, wv_ref, wqk_ref, wvb_ref, wr_ref, walr_ref,
                   wal_ref, bal_ref, wg_ref, bg_ref, lng_ref, lnb_ref, ws_ref, bs_ref,
                   wpa_ref, ma_ref, gb_ref, qin_ref, kin_ref, qout_ref, kst_ref, dec_ref,
                   vb_ref, rs_ref):
    tm, d = x_ref.shape
    for s in range(MIX_SUBBLOCKS):
        _mix_in_rows(slice(s * (tm // MIX_SUBBLOCKS), (s + 1) * (tm // MIX_SUBBLOCKS)),
                     x_ref, g_ref, wu_ref, wv_ref, wqk_ref, wvb_ref, wr_ref, walr_ref,
                     wal_ref, bal_ref, wg_ref, bg_ref, lng_ref, lnb_ref, ws_ref, bs_ref,
                     wpa_ref, ma_ref, gb_ref, qin_ref, kin_ref, qout_ref, kst_ref, dec_ref,
                     vb_ref, rs_ref)


def _mix_in_rows(rows, x_ref, g_ref, wu_ref, wv_ref, wqk_ref, wvb_ref, wr_ref, walr_ref,
                 wal_ref, bal_ref, wg_ref, bg_ref, lng_ref, lnb_ref, ws_ref, bs_ref,
                 wpa_ref, ma_ref, gb_ref, qin_ref, kin_ref, qout_ref, kst_ref, dec_ref,
                 vb_ref, rs_ref):
    tm = rows.stop - rows.start
    d = x_ref.shape[1]
    h = _rms(x_ref[rows, :], g_ref[...]).astype(BF16)

    alr = _dot(h, walr_ref[...]).astype(BF16)
    vb_ref[rows, :] = _dot(h, wvb_ref[...]).astype(BF16)
    pre = _dot(alr, wal_ref[...]) + bal_ref[...]
    r = _dot(h, wr_ref[...])
    rs_ref[rows, :] = (r * jax.nn.sigmoid(r)).astype(BF16)
    zv = _dot(h, wv_ref[...])
    gates = jax.nn.sigmoid(_dot(h, wg_ref[...]) + bg_ref[...])
    gb_ref[rows, :] = gates[:, d:].astype(BF16)
    qk = _dot(h, wqk_ref[...])
    zu = _dot(h, wu_ref[...])

    la = jax.nn.log_sigmoid(pre) / B_GATE_TAU
    kd = la.shape[1]
    nc = tm // B_CHUNK
    pos = lax.broadcasted_iota(jnp.int32, la.shape, 0) % B_CHUNK
    bc = la
    shift = 1
    while shift < B_CHUNK:
        bc = bc + jnp.where(pos >= shift, pltpu.roll(bc, shift, 0), 0.0)
        shift *= 2
    bc = bc.reshape(nc, B_CHUNK, kd)
    b_mid = bc[:, B_CHUNK // 2 - 1:B_CHUNK // 2, :]
    b_last = bc[:, B_CHUNK - 1:B_CHUNK, :]
    chunks = slice(rows.start // B_CHUNK, rows.stop // B_CHUNK)
    dec_ref[chunks, :] = jnp.exp(b_last).reshape(nc, kd)

    q = qk[:, :kd].reshape(nc, B_CHUNK, kd) * (B_HEAD_K ** -0.5)
    k = qk[:, kd:].reshape(nc, B_CHUNK, kd)
    qin_ref[rows, :] = (q * jnp.exp(bc - b_mid)).astype(BF16).reshape(tm, kd)
    kin_ref[rows, :] = (k * jnp.exp(b_mid - bc)).astype(BF16).reshape(tm, kd)
    qout_ref[rows, :] = (q * jnp.exp(bc)).astype(BF16).reshape(tm, kd)
    kst_ref[rows, :] = (k * jnp.exp(b_last - bc)).astype(BF16).reshape(tm, kd)

    v = _gelu(zv)
    mu = jnp.mean(v, axis=-1, keepdims=True)
    vc = v - mu
    var = jnp.mean(vc * vc, axis=-1, keepdims=True)
    vn = (vc * lax.rsqrt(var + EPS) * lng_ref[...] + lnb_ref[...]).astype(BF16)
    width = vn.shape[1]
    gdim = width // A_GROUPS
    sp_rows = []
    for n in range(tm // A_CHUNK):
        blocks = []
        for g in range(A_GROUPS):
            vblk = vn[n * A_CHUNK:(n + 1) * A_CHUNK, g * gdim:(g + 1) * gdim]
            blocks.append(_dot(ws_ref[g], vblk))
        sp_rows.append(jnp.concatenate(blocks, axis=1) + bs_ref[...])
    sp = jnp.concatenate(sp_rows, axis=0)
    ya = _dot((_gelu(zu) * sp).astype(BF16), wpa_ref[...])
    ma_ref[rows, :] = (gates[:, :d] * ya).astype(BF16)


def _mix_in(x1, p):
    t, d = x1.shape
    tm = MIX_TOKENS

    def tok(width):
        return pl.BlockSpec((tm, width), lambda i: (i, 0))

    weights = [p["mix_norm"], p["wu"], p["wv"], p["wqk"], p["wvb"], p["wr"], p["walr"],
               p["wal"], p["b_alpha"], p["wg"], p["b_gate"], p["ln_g"], p["ln_b"], p["ws"],
               p["bs"], p["wpa"]]
    kd = p["wqk"].shape[1] // 2
    vd = p["wvb"].shape[1]
    out_widths = [d, d, kd, kd, kd, kd, vd, vd]
    out_specs = [tok(w) for w in out_widths]
    out_shape = [jax.ShapeDtypeStruct((t, w), BF16) for w in out_widths]
    out_specs.insert(6, pl.BlockSpec((tm // B_CHUNK, kd), lambda i: (i, 0)))
    out_shape.insert(6, jax.ShapeDtypeStruct((t // B_CHUNK, kd), F32))
    return pl.pallas_call(
        _mix_in_kernel,
        grid=(t // tm,),
        in_specs=[tok(d)] + [_resident(w.shape) for w in weights],
        out_specs=out_specs,
        out_shape=out_shape,
        compiler_params=pltpu.CompilerParams(
            dimension_semantics=("arbitrary",), vmem_limit_bytes=VMEM_LIMIT_BYTES),
        name="mix_in",
    )(x1, *weights)


def _gla_kernel(qin_ref, kin_ref, qout_ref, kst_ref, dec_ref, vb_ref, rs_ref, gb_ref,
                ma_ref, x1_ref, hg_ref, wpb_ref, wo_ref, o_ref, st_ref, oacc_ref):
    ts, kd = qin_ref.shape
    nc = ts // B_CHUNK

    @pl.when(pl.program_id(1) == 0)
    def _():
        st_ref[...] = jnp.zeros_like(st_ref)

    causal = (lax.broadcasted_iota(jnp.int32, (B_CHUNK, B_CHUNK), 0)
              >= lax.broadcasted_iota(jnp.int32, (B_CHUNK, B_CHUNK), 1))

    pairs = [(c, hd) for c in range(nc) for hd in range(B_HEADS)]

    def rows(c):
        return slice(c * B_CHUNK, (c + 1) * B_CHUNK)

    def kcols(hd):
        return slice(hd * B_HEAD_K, (hd + 1) * B_HEAD_K)

    def vcols(hd):
        return slice(hd * B_HEAD_V, (hd + 1) * B_HEAD_V)

    scores, kv = {}, {}
    for c, hd in pairs:
        s = _dot_nt(qin_ref[rows(c), kcols(hd)], kin_ref[rows(c), kcols(hd)])
        scores[c, hd] = jnp.where(causal, s, 0.0).astype(BF16)
        kv[c, hd] = _dot_tn(vb_ref[rows(c), vcols(hd)], kst_ref[rows(c), kcols(hd)])
    for c, hd in pairs:
        oacc_ref[rows(c), vcols(hd)] = _dot(scores[c, hd], vb_ref[rows(c), vcols(hd)])
    state_t = [st_ref[hd] for hd in range(B_HEADS)]
    for c, hd in pairs:
        oacc_ref[rows(c), vcols(hd)] += _dot_nt(qout_ref[rows(c), kcols(hd)],
                                                state_t[hd].astype(BF16))
        state_t[hd] = dec_ref[c:c + 1, kcols(hd)] * state_t[hd] + kv[c, hd]
    for hd in range(B_HEADS):
        st_ref[hd] = state_t[hd]

    rs = rs_ref[...].astype(F32)
    parts = []
    for hd in range(B_HEADS):
        vc = slice(hd * B_HEAD_V, (hd + 1) * B_HEAD_V)
        parts.append(_rms(oacc_ref[:, vc], hg_ref[:, vc]) * rs[:, vc])
    yb = _dot(jnp.concatenate(parts, axis=1).astype(BF16), wpb_ref[...])
    merged = gb_ref[...].astype(F32) * yb + ma_ref[...].astype(F32)
    o_ref[...] = x1_ref[...] + _dot(merged.astype(BF16), wo_ref[...])


def _gla(qin, kin, qout, kst, dec, vb, rs, gb, ma, x1, p, batch, seq):
    t, d = x1.shape
    ts = GLA_TOKENS
    per_seq = seq // ts
    kd = qin.shape[1]
    vd = vb.shape[1]

    def tok(width, rows=ts):
        return pl.BlockSpec((rows, width), lambda b, s: (b * per_seq + s, 0))

    weights = [p["head_g"], p["wpb"], p["wo"]]
    return pl.pallas_call(
        _gla_kernel,
        grid=(batch, per_seq),
        in_specs=[tok(kd), tok(kd), tok(kd), tok(kd), tok(kd, ts // B_CHUNK), tok(vd),
                  tok(vd), tok(d), tok(d), tok(d)] + [_resident(w.shape) for w in weights],
        out_specs=tok(d),
        out_shape=jax.ShapeDtypeStruct((t, d), F32),
        scratch_shapes=[
            pltpu.VMEM((B_HEADS, B_HEAD_V, B_HEAD_K), F32),
            pltpu.VMEM((ts, vd), F32),
        ],
        compiler_params=pltpu.CompilerParams(
            dimension_semantics=("arbitrary", "arbitrary"),
            vmem_limit_bytes=VMEM_LIMIT_BYTES),
        name="gla",
    )(qin, kin, qout, kst, dec, vb, rs, gb, ma, x1, *weights)


def kernel(x, ffn1_norm, ffn1_w_in, ffn1_w_out, mix_norm, w_in, b_gate, a_ln_gain, a_ln_bias, a_w_s, a_b_s, b_w_alpha, b_b_alpha, b_head_norm, w_proj_a, w_proj_b, w_o, ffn2_norm, ffn2_w_in, ffn2_w_out, final_norm):
    batch, seq, d = x.shape
    depth = ffn1_norm.shape[0]
    a_width = a_ln_gain.shape[1]
    kd = b_w_alpha.shape[2]
    vd = w_proj_b.shape[1]
    assert seq % max(FFN_TOKENS, MIX_TOKENS, GLA_TOKENS) == 0
    assert MIX_TOKENS % (MIX_SUBBLOCKS * A_CHUNK) == 0 and GLA_TOKENS % B_CHUNK == 0
    xt = x.reshape(batch * seq, d)

    tril = jnp.tril(jnp.ones((A_CHUNK, A_CHUNK), dtype=bool))
    for l in range(depth):
        xt = _ffn(xt, ffn1_norm[l], ffn1_w_in[l], ffn1_w_out[l], final_norm, final=False)

        w = w_in[l].astype(BF16)
        o_u, o_v, o_q = 0, a_width, 2 * a_width
        o_vb = o_q + 2 * kd
        o_r = o_vb + vd
        o_alr = o_r + vd
        o_g = o_alr + B_GATE_RANK
        pad = LANES - B_GATE_RANK
        p = {
            "mix_norm": mix_norm[l].reshape(1, d),
            "wu": w[:, o_u:o_v], "wv": w[:, o_v:o_q], "wqk": w[:, o_q:o_vb],
            "wvb": w[:, o_vb:o_r], "wr": w[:, o_r:o_alr],
            "walr": jnp.pad(w[:, o_alr:o_g], ((0, 0), (0, pad))),
            "wg": w[:, o_g:], "b_gate": b_gate[l].reshape(1, 2 * d),
            "ln_g": a_ln_gain[l].reshape(1, a_width), "ln_b": a_ln_bias[l].reshape(1, a_width),
            "ws": jnp.where(tril[None], a_w_s[l], 0.0).astype(BF16),
            "bs": jnp.repeat(a_b_s[l].T, a_width // A_GROUPS, axis=1),
            "wpa": w_proj_a[l].astype(BF16),
            "wal": jnp.pad(b_w_alpha[l].astype(BF16), ((0, pad), (0, 0))),
            "b_alpha": b_b_alpha[l].reshape(1, kd),
            "head_g": b_head_norm[l].reshape(1, vd),
            "wpb": w_proj_b[l].astype(BF16), "wo": w_o[l].astype(BF16),
        }
        ma, gb, qin, kin, qout, kst, dec, vb, rs = _mix_in(xt, p)
        xt = _gla(qin, kin, qout, kst, dec, vb, rs, gb, ma, xt, p, batch, seq)

        last = l == depth - 1
        xt = _ffn(xt, ffn2_norm[l], ffn2_w_in[l], ffn2_w_out[l], final_norm, final=last)
    return xt.reshape(batch, seq, d)
```

```python
import jax
import jax.numpy as jnp
from jax import lax
from jax.experimental import pallas as pl
from jax.experimental.pallas import tpu as pltpu

F32 = jnp.float32
BF16 = jnp.bfloat16

EPS = 1e-6
LOG2_E = 1.4426950408889634
A_GROUPS = 8
A_CHUNK = 128
B_HEADS = 4
B_HEAD_K = 128
B_HEAD_V = 256
B_GATE_RANK = 16
B_GATE_TAU = 16.0
B_CHUNK = 64
LANES = 128
SUBLANES = 8

FFN_TOKENS = 1024
FFN_SUBBLOCKS = 4
MIX_TOKENS = 512
GLA_TOKENS = 512
VMEM_LIMIT_BYTES = 56 * 1024 * 1024


def _resident(shape):
    zeros = (0,) * len(shape)
    return pl.BlockSpec(shape, lambda *_: zeros, pipeline_mode=pl.Buffered(1))


def _rms(x, gain):
    return x * lax.rsqrt(jnp.mean(x * x, axis=-1, keepdims=True) + EPS) * gain


def _dot(a, b):
    return jnp.dot(a, b, preferred_element_type=F32)


def _dot_nt(a, b):
    return lax.dot_general(a, b, (((1,), (1,)), ((), ())), preferred_element_type=F32)


def _dot_tn(a, b):
    return lax.dot_general(a, b, (((0,), (0,)), ((), ())), preferred_element_type=F32)


def _swiglu_rows(x, g_ref, wab_ref, wo_ref):
    d_ff = wo_ref.shape[0]
    xn = _rms(x, g_ref[...]).astype(BF16)
    a = _dot(xn, wab_ref[:, :d_ff])
    b = _dot(xn, wab_ref[:, d_ff:])
    hidden = (a * jax.nn.sigmoid(a) * b).astype(BF16)
    return x + 0.5 * _dot(hidden, wo_ref[...])


def _sub_rows(ref, s):
    sub = ref.shape[0] // FFN_SUBBLOCKS
    return slice(s * sub, (s + 1) * sub)


def _ffn_mix_kernel(x_ref, g_ref, wab_ref, wo_ref, mg_ref, x1_ref, h_ref):
    for s in range(FFN_SUBBLOCKS):
        rows = _sub_rows(x_ref, s)
        x1 = _swiglu_rows(x_ref[rows, :], g_ref, wab_ref, wo_ref)
        x1_ref[rows, :] = x1
        h_ref[rows, :] = _rms(x1, mg_ref[...]).astype(BF16)


def _ffn_final_kernel(x_ref, g_ref, wab_ref, wo_ref, fg_ref, o_ref):
    for s in range(FFN_SUBBLOCKS):
        rows = _sub_rows(x_ref, s)
        y = _swiglu_rows(x_ref[rows, :], g_ref, wab_ref, wo_ref)
        o_ref[rows, :] = _rms(y, fg_ref[...])


def _ffn_call(kernel_fn, name, x, gain, w_in, w_out, out_gain, out_dtypes):
    t, d = x.shape
    d_ff = w_out.shape[0]
    tok = pl.BlockSpec((FFN_TOKENS, d), lambda i: (i, 0))
    return pl.pallas_call(
        kernel_fn,
        grid=(t // FFN_TOKENS,),
        in_specs=[tok, _resident((1, d)), _resident((d, 2 * d_ff)), _resident((d_ff, d)),
                  _resident((1, d))],
        out_specs=[tok] * len(out_dtypes),
        out_shape=[jax.ShapeDtypeStruct((t, d), dt) for dt in out_dtypes],
        compiler_params=pltpu.CompilerParams(
            dimension_semantics=("arbitrary",), vmem_limit_bytes=VMEM_LIMIT_BYTES),
        name=name,
    )(x, gain.reshape(1, d), w_in.astype(BF16), w_out.astype(BF16), out_gain.reshape(1, d))


def _gelu(x):
    return 0.5 * x * (1.0 + lax.erf(x * (2.0 ** -0.5)))


def _log2_decay(pre):
    log_sig = jnp.minimum(pre, 0.0) - jnp.log(1.0 + jnp.exp(-jnp.abs(pre)))
    return log_sig * (LOG2_E / B_GATE_TAU)


def _chunk_cumsum(x):
    rows, width = x.shape
    groups = B_CHUNK // SUBLANES
    x = x.reshape(rows // SUBLANES, SUBLANES, width)
    pos = lax.broadcasted_iota(jnp.int32, x.shape, 1)
    shift = 1
    while shift < SUBLANES:
        x = x + jnp.where(pos >= shift, pltpu.roll(x, shift, 1), 0.0)
        shift *= 2
    x = x.reshape(rows // B_CHUNK, groups, SUBLANES, width)
    out = [x[:, 0]]
    for g in range(1, groups):
        total = out[-1][:, SUBLANES - 1:SUBLANES, :]
        out.append(x[:, g] + total)
    return jnp.stack(out, axis=1).reshape(rows, width)


def _mix_in_kernel(h_ref, w_ref, wal_ref, bal_ref, bg_ref, lng_ref, lnb_ref, ws_ref, bs_ref,
                   wpa_ref, ma_ref, gb_ref, qin_ref, kin_ref, qout_ref, kst_ref, dec_ref,
                   vb_ref, rs_ref):
    tm, d = h_ref.shape
    kd = qin_ref.shape[1]
    h = h_ref[...]

    cols = {}
    start = 0
    for name, width in (("u", lng_ref.shape[1]), ("v", lng_ref.shape[1]), ("qk", 2 * kd),
                        ("vb", vb_ref.shape[1]), ("r", rs_ref.shape[1]), ("alr", LANES),
                        ("g", 2 * d)):
        cols[name] = slice(start, start + width)
        start += width

    alr = _dot(h, w_ref[:, cols["alr"]]).astype(BF16)
    vb_ref[...] = _dot(h, w_ref[:, cols["vb"]]).astype(BF16)
    pre = _dot(alr, wal_ref[...]) + bal_ref[...]
    r = _dot(h, w_ref[:, cols["r"]])
    rs_ref[...] = (r * jax.nn.sigmoid(r)).astype(BF16)
    zv = _dot(h, w_ref[:, cols["v"]])
    gates = jax.nn.sigmoid(_dot(h, w_ref[:, cols["g"]]) + bg_ref[...])
    gb_ref[...] = gates[:, d:].astype(BF16)
    qk = _dot(h, w_ref[:, cols["qk"]])
    zu = _dot(h, w_ref[:, cols["u"]])

    nc = tm // B_CHUNK
    bc = _chunk_cumsum(_log2_decay(pre)).reshape(nc, B_CHUNK, kd)
    b_mid = bc[:, B_CHUNK // 2 - 1:B_CHUNK // 2, :]
    b_last = bc[:, B_CHUNK - 1:B_CHUNK, :]
    dec_ref[...] = jnp.exp2(b_last).reshape(nc, kd)

    q = qk[:, :kd].reshape(nc, B_CHUNK, kd) * (B_HEAD_K ** -0.5)
    k = qk[:, kd:].reshape(nc, B_CHUNK, kd)
    qin_ref[...] = (q * jnp.exp2(bc - b_mid)).astype(BF16).reshape(tm, kd)
    kin_ref[...] = (k * jnp.exp2(b_mid - bc)).astype(BF16).reshape(tm, kd)
    qout_ref[...] = (q * jnp.exp2(bc)).astype(BF16).reshape(tm, kd)
    kst_ref[...] = (k * jnp.exp2(b_last - bc)).astype(BF16).reshape(tm, kd)

    v = _gelu(zv)
    mu = jnp.mean(v, axis=-1, keepdims=True)
    vc = v - mu
    var = jnp.mean(vc * vc, axis=-1, keepdims=True)
    vn = (vc * lax.rsqrt(var + EPS) * lng_ref[...] + lnb_ref[...]).astype(BF16)
    width = vn.shape[1]
    gdim = width // A_GROUPS
    sp_rows = []
    for n in range(tm // A_CHUNK):
        blocks = []
        for g in range(A_GROUPS):
            vblk = vn[n * A_CHUNK:(n + 1) * A_CHUNK, g * gdim:(g + 1) * gdim]
            blocks.append(_dot(ws_ref[g], vblk))
        sp_rows.append(jnp.concatenate(blocks, axis=1) + bs_ref[...])
    sp = jnp.concatenate(sp_rows, axis=0)
    ya = _dot((_gelu(zu) * sp).astype(BF16), wpa_ref[...])
    ma_ref[...] = (gates[:, :d] * ya).astype(BF16)


def _mix_in(h, p):
    t, d = h.shape
    tm = MIX_TOKENS

    def tok(width):
        return pl.BlockSpec((tm, width), lambda i: (i, 0))

    weights = [p["w"], p["wal"], p["b_alpha"], p["b_gate"], p["ln_g"], p["ln_b"], p["ws"],
               p["bs"], p["wpa"]]
    kd = p["wal"].shape[1]
    vd = p["wpb"].shape[0]
    out_widths = [d, d, kd, kd, kd, kd, vd, vd]
    out_specs = [tok(w) for w in out_widths]
    out_shape = [jax.ShapeDtypeStruct((t, w), BF16) for w in out_widths]
    out_specs.insert(6, pl.BlockSpec((tm // B_CHUNK, kd), lambda i: (i, 0)))
    out_shape.insert(6, jax.ShapeDtypeStruct((t // B_CHUNK, kd), F32))
    return pl.pallas_call(
        _mix_in_kernel,
        grid=(t // tm,),
        in_specs=[tok(d)] + [_resident(w.shape) for w in weights],
        out_specs=out_specs,
        out_shape=out_shape,
        compiler_params=pltpu.CompilerParams(
            dimension_semantics=("arbitrary",), vmem_limit_bytes=VMEM_LIMIT_BYTES),
        name="mix_in",
    )(h, *weights)


def _gla_kernel(qin_ref, kin_ref, qout_ref, kst_ref, dec_ref, vb_ref, rs_ref, gb_ref,
                ma_ref, x1_ref, hg_ref, wpb_ref, wo_ref, o_ref, st_ref, oacc_ref):
    ts, kd = qin_ref.shape
    nc = ts // B_CHUNK

    @pl.when(pl.program_id(1) == 0)
    def _():
        st_ref[...] = jnp.zeros_like(st_ref)

    causal = (lax.broadcasted_iota(jnp.int32, (B_CHUNK, B_CHUNK), 0)
              >= lax.broadcasted_iota(jnp.int32, (B_CHUNK, B_CHUNK), 1))

    pairs = [(c, hd) for c in range(nc) for hd in range(B_HEADS)]

    def rows(c):
        return slice(c * B_CHUNK, (c + 1) * B_CHUNK)

    def kcols(hd):
        return slice(hd * B_HEAD_K, (hd + 1) * B_HEAD_K)

    def vcols(hd):
        return slice(hd * B_HEAD_V, (hd + 1) * B_HEAD_V)

    scores, kv = {}, {}
    for c, hd in pairs:
        s = _dot_nt(qin_ref[rows(c), kcols(hd)], kin_ref[rows(c), kcols(hd)])
        scores[c, hd] = jnp.where(causal, s, 0.0).astype(BF16)
        kv[c, hd] = _dot_tn(vb_ref[rows(c), vcols(hd)], kst_ref[rows(c), kcols(hd)])
    for c, hd in pairs:
        oacc_ref[rows(c), vcols(hd)] = _dot(scores[c, hd], vb_ref[rows(c), vcols(hd)])
    state_t = [st_ref[hd] for hd in range(B_HEADS)]
    for c, hd in pairs:
        oacc_ref[rows(c), vcols(hd)] += _dot_nt(qout_ref[rows(c), kcols(hd)],
                                                state_t[hd].astype(BF16))
        state_t[hd] = dec_ref[c:c + 1, kcols(hd)] * state_t[hd] + kv[c, hd]
    for hd in range(B_HEADS):
        st_ref[hd] = state_t[hd]

    rs = rs_ref[...].astype(F32)
    parts = []
    for hd in range(B_HEADS):
        vc = slice(hd * B_HEAD_V, (hd + 1) * B_HEAD_V)
        parts.append(_rms(oacc_ref[:, vc], hg_ref[:, vc]) * rs[:, vc])
    yb = _dot(jnp.concatenate(parts, axis=1).astype(BF16), wpb_ref[...])
    merged = gb_ref[...].astype(F32) * yb + ma_ref[...].astype(F32)
    o_ref[...] = x1_ref[...] + _dot(merged.astype(BF16), wo_ref[...])


def _gla(qin, kin, qout, kst, dec, vb, rs, gb, ma, x1, p, batch, seq):
    t, d = x1.shape
    ts = GLA_TOKENS
    per_seq = seq // ts
    kd = qin.shape[1]
    vd = vb.shape[1]

    def tok(width, rows=ts):
        return pl.BlockSpec((rows, width), lambda b, s: (b * per_seq + s, 0))

    weights = [p["head_g"], p["wpb"], p["wo"]]
    return pl.pallas_call(
        _gla_kernel,
        grid=(batch, per_seq),
        in_specs=[tok(kd), tok(kd), tok(kd), tok(kd), tok(kd, ts // B_CHUNK), tok(vd),
                  tok(vd), tok(d), tok(d), tok(d)] + [_resident(w.shape) for w in weights],
        out_specs=tok(d),
        out_shape=jax.ShapeDtypeStruct((t, d), F32),
        scratch_shapes=[
            pltpu.VMEM((B_HEADS, B_HEAD_V, B_HEAD_K), F32),
            pltpu.VMEM((ts, vd), F32),
        ],
        compiler_params=pltpu.CompilerParams(
            dimension_semantics=("arbitrary", "arbitrary"),
            vmem_limit_bytes=VMEM_LIMIT_BYTES),
        name="gla",
    )(qin, kin, qout, kst, dec, vb, rs, gb, ma, x1, *weights)


def kernel(x, ffn1_norm, ffn1_w_in, ffn1_w_out, mix_norm, w_in, b_gate, a_ln_gain, a_ln_bias, a_w_s, a_b_s, b_w_alpha, b_b_alpha, b_head_norm, w_proj_a, w_proj_b, w_o, ffn2_norm, ffn2_w_in, ffn2_w_out, final_norm):
    batch, seq, d = x.shape
    a_width = a_ln_gain.shape[1]
    kd = b_w_alpha.shape[2]
    vd = w_proj_b.shape[1]
    assert ffn1_norm.shape[0] == 1, "one layer: the final norm is fused into its last half step"
    assert seq % max(FFN_TOKENS, MIX_TOKENS, GLA_TOKENS) == 0
    assert MIX_TOKENS % A_CHUNK == 0 and GLA_TOKENS % B_CHUNK == 0
    xt = x.reshape(batch * seq, d)

    o_g = 2 * a_width + 2 * kd + 2 * vd + B_GATE_RANK
    pad = LANES - B_GATE_RANK
    w = jnp.concatenate([w_in[0][:, :o_g], jnp.zeros((d, pad), w_in.dtype), w_in[0][:, o_g:]],
                        axis=1).astype(BF16)
    tril = jnp.tril(jnp.ones((A_CHUNK, A_CHUNK), dtype=bool))
    p = {
        "w": w, "b_gate": b_gate[0].reshape(1, 2 * d),
        "ln_g": a_ln_gain[0].reshape(1, a_width), "ln_b": a_ln_bias[0].reshape(1, a_width),
        "ws": jnp.where(tril[None], a_w_s[0], 0.0).astype(BF16),
        "bs": jnp.repeat(a_b_s[0].T, a_width // A_GROUPS, axis=1),
        "wpa": w_proj_a[0].astype(BF16),
        "wal": jnp.pad(b_w_alpha[0].astype(BF16), ((0, pad), (0, 0))),
        "b_alpha": b_b_alpha[0].reshape(1, kd),
        "head_g": b_head_norm[0].reshape(1, vd),
        "wpb": w_proj_b[0].astype(BF16), "wo": w_o[0].astype(BF16),
    }
    x1, h = _ffn_call(_ffn_mix_kernel, "ffn_mix", xt, ffn1_norm[0], ffn1_w_in[0],
                      ffn1_w_out[0], mix_norm[0], [F32, BF16])
    ma, gb, qin, kin, qout, kst, dec, vb, rs = _mix_in(h, p)
    x2 = _gla(qin, kin, qout, kst, dec, vb, rs, gb, ma, x1, p, batch, seq)
    out, = _ffn_call(_ffn_final_kernel, "ffn_final", x2, ffn2_norm[0], ffn2_w_in[0],
                     ffn2_w_out[0], final_norm, [F32])
    return out.reshape(batch, seq, d)
```

```python
import jax
import jax.numpy as jnp
from jax import lax
from jax.experimental import pallas as pl
from jax.experimental.pallas import tpu as pltpu

F32 = jnp.float32
BF16 = jnp.bfloat16

EPS = 1e-6
LOG2_E = 1.4426950408889634
A_GROUPS = 8
A_CHUNK = 128
B_HEADS = 4
B_HEAD_K = 128
B_HEAD_V = 256
B_GATE_RANK = 16
B_GATE_TAU = 16.0
B_CHUNK = 64
LANES = 128
SUBLANES = 8

FFN_TOKENS = 1024
FFN_SUBBLOCKS = 4
MIX_TOKENS = 512
GLA_TOKENS = 512
VMEM_LIMIT_BYTES = 56 * 1024 * 1024


def _resident(shape):
    zeros = (0,) * len(shape)
    return pl.BlockSpec(shape, lambda *_: zeros, pipeline_mode=pl.Buffered(1))


def _rms(x, gain):
    return x * lax.rsqrt(jnp.mean(x * x, axis=-1, keepdims=True) + EPS) * gain


def _dot(a, b):
    return jnp.dot(a, b, preferred_element_type=F32)


def _dot_nt(a, b):
    return lax.dot_general(a, b, (((1,), (1,)), ((), ())), preferred_element_type=F32)


def _dot_tn(a, b):
    return lax.dot_general(a, b, (((0,), (0,)), ((), ())), preferred_element_type=F32)


def _swiglu_rows(x, g_ref, wab_ref, wo_ref):
    d_ff = wo_ref.shape[0]
    xn = _rms(x, g_ref[...]).astype(BF16)
    a = _dot(xn, wab_ref[:, :d_ff])
    b = _dot(xn, wab_ref[:, d_ff:])
    hidden = (a * jax.nn.sigmoid(a) * b).astype(BF16)
    return x + 0.5 * _dot(hidden, wo_ref[...])


def _sub_rows(ref, s):
    sub = ref.shape[0] // FFN_SUBBLOCKS
    return slice(s * sub, (s + 1) * sub)


def _ffn_mix_kernel(x_ref, g_ref, wab_ref, wo_ref, mg_ref, x1_ref, h_ref):
    for s in range(FFN_SUBBLOCKS):
        rows = _sub_rows(x_ref, s)
        x1 = _swiglu_rows(x_ref[rows, :], g_ref, wab_ref, wo_ref)
        x1_ref[rows, :] = x1
        h_ref[rows, :] = _rms(x1, mg_ref[...]).astype(BF16)


def _ffn_final_kernel(x_ref, g_ref, wab_ref, wo_ref, fg_ref, o_ref):
    for s in range(FFN_SUBBLOCKS):
        rows = _sub_rows(x_ref, s)
        y = _swiglu_rows(x_ref[rows, :], g_ref, wab_ref, wo_ref)
        o_ref[rows, :] = _rms(y, fg_ref[...])


def _ffn_call(kernel_fn, name, x, gain, w_in, w_out, out_gain, out_dtypes):
    t, d = x.shape
    d_ff = w_out.shape[0]
    tok = pl.BlockSpec((FFN_TOKENS, d), lambda i: (i, 0))
    return pl.pallas_call(
        kernel_fn,
        grid=(t // FFN_TOKENS,),
        in_specs=[tok, _resident((1, d)), _resident((d, 2 * d_ff)), _resident((d_ff, d)),
                  _resident((1, d))],
        out_specs=[tok] * len(out_dtypes),
        out_shape=[jax.ShapeDtypeStruct((t, d), dt) for dt in out_dtypes],
        compiler_params=pltpu.CompilerParams(
            dimension_semantics=("arbitrary",), vmem_limit_bytes=VMEM_LIMIT_BYTES),
        name=name,
    )(x, gain.reshape(1, d), w_in.astype(BF16), w_out.astype(BF16), out_gain.reshape(1, d))


def _gelu(x):
    return 0.5 * x * (1.0 + lax.erf(x * (2.0 ** -0.5)))


def _log2_decay(pre):
    log_sig = jnp.minimum(pre, 0.0) - jnp.log(1.0 + jnp.exp(-jnp.abs(pre)))
    return log_sig * (LOG2_E / B_GATE_TAU)


def _chunk_cumsum(x):
    rows, width = x.shape
    groups = B_CHUNK // SUBLANES
    x = x.reshape(rows // SUBLANES, SUBLANES, width)
    pos = lax.broadcasted_iota(jnp.int32, x.shape, 1)
    shift = 1
    while shift < SUBLANES:
        x = x + jnp.where(pos >= shift, pltpu.roll(x, shift, 1), 0.0)
        shift *= 2
    x = x.reshape(rows // B_CHUNK, groups, SUBLANES, width)
    out = [x[:, 0]]
    for g in range(1, groups):
        total = out[-1][:, SUBLANES - 1:SUBLANES, :]
        out.append(x[:, g] + total)
    return jnp.stack(out, axis=1).reshape(rows, width)


def _mix_in_kernel(h_ref, w_ref, wg_ref, wal_ref, bal_ref, bg_ref, lng_ref, lnb_ref, ws_ref,
                   bs_ref, wpa_ref, ma_ref, gb_ref, qin_ref, kin_ref, qout_ref, kst_ref,
                   dec_ref, vb_ref, rs_ref):
    tm, d = h_ref.shape
    kd = qin_ref.shape[1]
    h = h_ref[...]

    cols = {}
    start = 0
    for name, width in (("u", lng_ref.shape[1]), ("v", lng_ref.shape[1]), ("qk", 2 * kd),
                        ("vb", vb_ref.shape[1]), ("r", rs_ref.shape[1]), ("alr", LANES)):
        cols[name] = slice(start, start + width)
        start += width

    alr = _dot(h, w_ref[:, cols["alr"]])
    rank_lanes = lax.broadcasted_iota(jnp.int32, alr.shape, 1) < B_GATE_RANK
    alr = jnp.where(rank_lanes, alr, 0.0).astype(BF16)
    vb_ref[...] = _dot(h, w_ref[:, cols["vb"]]).astype(BF16)
    pre = _dot(alr, wal_ref[...]) + bal_ref[...]
    r = _dot(h, w_ref[:, cols["r"]])
    rs_ref[...] = (r * jax.nn.sigmoid(r)).astype(BF16)
    zv = _dot(h, w_ref[:, cols["v"]])
    gates = jax.nn.sigmoid(_dot(h, wg_ref[...]) + bg_ref[...])
    gb_ref[...] = gates[:, d:].astype(BF16)
    qk = _dot(h, w_ref[:, cols["qk"]])
    zu = _dot(h, w_ref[:, cols["u"]])

    nc = tm // B_CHUNK
    bc = _chunk_cumsum(_log2_decay(pre)).reshape(nc, B_CHUNK, kd)
    b_mid = bc[:, B_CHUNK // 2 - 1:B_CHUNK // 2, :]
    b_last = bc[:, B_CHUNK - 1:B_CHUNK, :]
    dec_ref[...] = jnp.exp2(b_last).reshape(nc, kd)

    q = qk[:, :kd].reshape(nc, B_CHUNK, kd) * (B_HEAD_K ** -0.5)
    k = qk[:, kd:].reshape(nc, B_CHUNK, kd)
    qin_ref[...] = (q * jnp.exp2(bc - b_mid)).astype(BF16).reshape(tm, kd)
    kin_ref[...] = (k * jnp.exp2(b_mid - bc)).astype(BF16).reshape(tm, kd)
    qout_ref[...] = (q * jnp.exp2(bc)).astype(BF16).reshape(tm, kd)
    kst_ref[...] = (k * jnp.exp2(b_last - bc)).astype(BF16).reshape(tm, kd)

    v = _gelu(zv)
    mu = jnp.mean(v, axis=-1, keepdims=True)
    vc = v - mu
    var = jnp.mean(vc * vc, axis=-1, keepdims=True)
    vn = (vc * lax.rsqrt(var + EPS) * lng_ref[...] + lnb_ref[...]).astype(BF16)
    width = vn.shape[1]
    gdim = width // A_GROUPS
    sp_rows = []
    for n in range(tm // A_CHUNK):
        blocks = []
        for g in range(A_GROUPS):
            vblk = vn[n * A_CHUNK:(n + 1) * A_CHUNK, g * gdim:(g + 1) * gdim]
            blocks.append(_dot(ws_ref[g], vblk))
        sp_rows.append(jnp.concatenate(blocks, axis=1) + bs_ref[...])
    sp = jnp.concatenate(sp_rows, axis=0)
    ya = _dot((_gelu(zu) * sp).astype(BF16), wpa_ref[...])
    ma_ref[...] = (gates[:, :d] * ya).astype(BF16)


def _mix_in(h, p):
    t, d = h.shape
    tm = MIX_TOKENS

    def tok(width):
        return pl.BlockSpec((tm, width), lambda i: (i, 0))

    weights = [p["w"], p["wg"], p["wal"], p["b_alpha"], p["b_gate"], p["ln_g"], p["ln_b"], p["ws"],
               p["bs"], p["wpa"]]
    kd = p["wal"].shape[1]
    vd = p["wpb"].shape[0]
    out_widths = [d, d, kd, kd, kd, kd, vd, vd]
    out_specs = [tok(w) for w in out_widths]
    out_shape = [jax.ShapeDtypeStruct((t, w), BF16) for w in out_widths]
    out_specs.insert(6, pl.BlockSpec((tm // B_CHUNK, kd), lambda i: (i, 0)))
    out_shape.insert(6, jax.ShapeDtypeStruct((t // B_CHUNK, kd), F32))
    return pl.pallas_call(
        _mix_in_kernel,
        grid=(t // tm,),
        in_specs=[tok(d)] + [_resident(w.shape) for w in weights],
        out_specs=out_specs,
        out_shape=out_shape,
        compiler_params=pltpu.CompilerParams(
            dimension_semantics=("arbitrary",), vmem_limit_bytes=VMEM_LIMIT_BYTES),
        name="mix_in",
    )(h, *weights)


def _gla_kernel(qin_ref, kin_ref, qout_ref, kst_ref, dec_ref, vb_ref, rs_ref, gb_ref,
                ma_ref, x1_ref, hg_ref, wpb_ref, wo_ref, o_ref, st_ref, oacc_ref):
    ts, kd = qin_ref.shape
    nc = ts // B_CHUNK

    @pl.when(pl.program_id(1) == 0)
    def _():
        st_ref[...] = jnp.zeros_like(st_ref)

    causal = (lax.broadcasted_iota(jnp.int32, (B_CHUNK, B_CHUNK), 0)
              >= lax.broadcasted_iota(jnp.int32, (B_CHUNK, B_CHUNK), 1))

    pairs = [(c, hd) for c in range(nc) for hd in range(B_HEADS)]

    def rows(c):
        return slice(c * B_CHUNK, (c + 1) * B_CHUNK)

    def kcols(hd):
        return slice(hd * B_HEAD_K, (hd + 1) * B_HEAD_K)

    def vcols(hd):
        return slice(hd * B_HEAD_V, (hd + 1) * B_HEAD_V)

    scores, kv = {}, {}
    for c, hd in pairs:
        s = _dot_nt(qin_ref[rows(c), kcols(hd)], kin_ref[rows(c), kcols(hd)])
        scores[c, hd] = jnp.where(causal, s, 0.0).astype(BF16)
        kv[c, hd] = _dot_tn(vb_ref[rows(c), vcols(hd)], kst_ref[rows(c), kcols(hd)])
    for c, hd in pairs:
        oacc_ref[rows(c), vcols(hd)] = _dot(scores[c, hd], vb_ref[rows(c), vcols(hd)])
    state_t = [st_ref[hd] for hd in range(B_HEADS)]
    for c, hd in pairs:
        oacc_ref[rows(c), vcols(hd)] += _dot_nt(qout_ref[rows(c), kcols(hd)],
                                                state_t[hd].astype(BF16))
        state_t[hd] = dec_ref[c:c + 1, kcols(hd)] * state_t[hd] + kv[c, hd]
    for hd in range(B_HEADS):
        st_ref[hd] = state_t[hd]

    rs = rs_ref[...].astype(F32)
    parts = []
    for hd in range(B_HEADS):
        parts.append(_rms(oacc_ref[:, vcols(hd)], hg_ref[:, vcols(hd)]) * rs[:, vcols(hd)])
    yb = _dot(jnp.concatenate(parts, axis=1).astype(BF16), wpb_ref[...])
    merged = gb_ref[...].astype(F32) * yb + ma_ref[...].astype(F32)
    o_ref[...] = x1_ref[...] + _dot(merged.astype(BF16), wo_ref[...])


def _gla(qin, kin, qout, kst, dec, vb, rs, gb, ma, x1, p, batch, seq):
    t, d = x1.shape
    ts = GLA_TOKENS
    per_seq = seq // ts
    kd = qin.shape[1]
    vd = vb.shape[1]

    def tok(width, rows=ts):
        return pl.BlockSpec((rows, width), lambda b, s: (b * per_seq + s, 0))

    weights = [p["head_g"], p["wpb"], p["wo"]]
    return pl.pallas_call(
        _gla_kernel,
        grid=(batch, per_seq),
        in_specs=[tok(kd), tok(kd), tok(kd), tok(kd), tok(kd, ts // B_CHUNK), tok(vd),
                  tok(vd), tok(d), tok(d), tok(d)] + [_resident(w.shape) for w in weights],
        out_specs=tok(d),
        out_shape=jax.ShapeDtypeStruct((t, d), F32),
        scratch_shapes=[
            pltpu.VMEM((B_HEADS, B_HEAD_V, B_HEAD_K), F32),
            pltpu.VMEM((ts, vd), F32),
        ],
        compiler_params=pltpu.CompilerParams(
            dimension_semantics=("arbitrary", "arbitrary"),
            vmem_limit_bytes=VMEM_LIMIT_BYTES),
        name="gla",
    )(qin, kin, qout, kst, dec, vb, rs, gb, ma, x1, *weights)


def kernel(x, ffn1_norm, ffn1_w_in, ffn1_w_out, mix_norm, w_in, b_gate, a_ln_gain, a_ln_bias, a_w_s, a_b_s, b_w_alpha, b_b_alpha, b_head_norm, w_proj_a, w_proj_b, w_o, ffn2_norm, ffn2_w_in, ffn2_w_out, final_norm):
    batch, seq, d = x.shape
    a_width = a_ln_gain.shape[1]
    kd = b_w_alpha.shape[2]
    vd = w_proj_b.shape[1]
    assert ffn1_norm.shape[0] == 1, "one layer: the final norm is fused into its last half step"
    assert seq % max(FFN_TOKENS, MIX_TOKENS, GLA_TOKENS) == 0
    assert MIX_TOKENS % A_CHUNK == 0 and GLA_TOKENS % B_CHUNK == 0
    xt = x.reshape(batch * seq, d)

    w = w_in[0].astype(BF16)
    o_g = 2 * a_width + 2 * kd + 2 * vd + B_GATE_RANK
    pad = LANES - B_GATE_RANK
    tril = jnp.tril(jnp.ones((A_CHUNK, A_CHUNK), dtype=bool))
    p = {
        "w": w, "wg": w[:, o_g:], "b_gate": b_gate[0].reshape(1, 2 * d),
        "ln_g": a_ln_gain[0].reshape(1, a_width), "ln_b": a_ln_bias[0].reshape(1, a_width),
        "ws": jnp.where(tril[None], a_w_s[0], 0.0).astype(BF16),
        "bs": jnp.repeat(a_b_s[0].T, a_width // A_GROUPS, axis=1),
        "wpa": w_proj_a[0].astype(BF16),
        "wal": jnp.pad(b_w_alpha[0].astype(BF16), ((0, pad), (0, 0))),
        "b_alpha": b_b_alpha[0].reshape(1, kd),
        "head_g": b_head_norm[0].reshape(1, vd),
        "wpb": w_proj_b[0].astype(BF16), "wo": w_o[0].astype(BF16),
    }
    x1, h = _ffn_call(_ffn_mix_kernel, "ffn_mix", xt, ffn1_norm[0], ffn1_w_in[0],
                      ffn1_w_out[0], mix_norm[0], [F32, BF16])
    ma, gb, qin, kin, qout, kst, dec, vb, rs = _mix_in(h, p)
    x2 = _gla(qin, kin, qout, kst, dec, vb, rs, gb, ma, x1, p, batch, seq)
    out, = _ffn_call(_ffn_final_kernel, "ffn_final", x2, ffn2_norm[0], ffn2_w_in[0],
                     ffn2_w_out[0], final_norm, [F32])
    return out.reshape(batch, seq, d)
```

```python
import jax
import jax.numpy as jnp
from jax import lax
from jax.experimental import pallas as pl
from jax.experimental.pallas import tpu as pltpu

F32 = jnp.float32
BF16 = jnp.bfloat16

EPS = 1e-6
LOG2_E = 1.4426950408889634
A_GROUPS = 8
A_CHUNK = 128
B_HEADS = 4
B_HEAD_K = 128
B_HEAD_V = 256
B_GATE_RANK = 16
B_GATE_TAU = 16.0
B_CHUNK = 64
LANES = 128
SUBLANES = 8

FFN_TOKENS = 1024
FFN_SUBBLOCKS = 4
MIX_TOKENS = 512
GLA_TOKENS = 512
VMEM_LIMIT_BYTES = 56 * 1024 * 1024


def _resident(shape):
    zeros = (0,) * len(shape)
    return pl.BlockSpec(shape, lambda *_: zeros, pipeline_mode=pl.Buffered(1))


def _rms(x, gain):
    return x * lax.rsqrt(jnp.mean(x * x, axis=-1, keepdims=True) + EPS) * gain


def _dot(a, b):
    return jnp.dot(a, b, preferred_element_type=F32)


def _dot_nt(a, b):
    return lax.dot_general(a, b, (((1,), (1,)), ((), ())), preferred_element_type=F32)


def _dot_tn(a, b):
    return lax.dot_general(a, b, (((0,), (0,)), ((), ())), preferred_element_type=F32)


def _swiglu_rows(x, g_ref, wab_ref, wo_ref):
    d_ff = wo_ref.shape[0]
    xn = _rms(x, g_ref[...]).astype(BF16)
    a = _dot(xn, wab_ref[:, :d_ff])
    b = _dot(xn, wab_ref[:, d_ff:])
    hidden = (a * jax.nn.sigmoid(a) * b).astype(BF16)
    return x + 0.5 * _dot(hidden, wo_ref[...])


def _sub_rows(ref, s):
    sub = ref.shape[0] // FFN_SUBBLOCKS
    return slice(s * sub, (s + 1) * sub)


def _ffn_mix_kernel(x_ref, g_ref, wab_ref, wo_ref, mg_ref, x1_ref, h_ref):
    for s in range(FFN_SUBBLOCKS):
        rows = _sub_rows(x_ref, s)
        x1 = _swiglu_rows(x_ref[rows, :], g_ref, wab_ref, wo_ref)
        x1_ref[rows, :] = x1
        h_ref[rows, :] = _rms(x1, mg_ref[...]).astype(BF16)


def _ffn_final_kernel(x_ref, g_ref, wab_ref, wo_ref, fg_ref, o_ref):
    for s in range(FFN_SUBBLOCKS):
        rows = _sub_rows(x_ref, s)
        y = _swiglu_rows(x_ref[rows, :], g_ref, wab_ref, wo_ref)
        o_ref[rows, :] = _rms(y, fg_ref[...])


def _ffn_call(kernel_fn, name, x, gain, w_in, w_out, out_gain, out_dtypes):
    t, d = x.shape
    d_ff = w_out.shape[0]
    tok = pl.BlockSpec((FFN_TOKENS, d), lambda i: (i, 0))
    return pl.pallas_call(
        kernel_fn,
        grid=(t // FFN_TOKENS,),
        in_specs=[tok, _resident((1, d)), _resident((d, 2 * d_ff)), _resident((d_ff, d)),
                  _resident((1, d))],
        out_specs=[tok] * len(out_dtypes),
        out_shape=[jax.ShapeDtypeStruct((t, d), dt) for dt in out_dtypes],
        compiler_params=pltpu.CompilerParams(
            dimension_semantics=("arbitrary",), vmem_limit_bytes=VMEM_LIMIT_BYTES),
        name=name,
    )(x, gain.reshape(1, d), w_in.astype(BF16), w_out.astype(BF16), out_gain.reshape(1, d))


def _gelu(x):
    return 0.5 * x * (1.0 + lax.erf(x * (2.0 ** -0.5)))


def _log2_decay(pre):
    log_sig = jnp.minimum(pre, 0.0) - jnp.log(1.0 + jnp.exp(-jnp.abs(pre)))
    return log_sig * (LOG2_E / B_GATE_TAU)


def _chunk_cumsum(x):
    rows, width = x.shape
    groups = B_CHUNK // SUBLANES
    x = x.reshape(rows // SUBLANES, SUBLANES, width)
    pos = lax.broadcasted_iota(jnp.int32, x.shape, 1)
    shift = 1
    while shift < SUBLANES:
        x = x + jnp.where(pos >= shift, pltpu.roll(x, shift, 1), 0.0)
        shift *= 2
    x = x.reshape(rows // B_CHUNK, groups, SUBLANES, width)
    out = [x[:, 0]]
    for g in range(1, groups):
        total = out[-1][:, SUBLANES - 1:SUBLANES, :]
        out.append(x[:, g] + total)
    return jnp.stack(out, axis=1).reshape(rows, width)


def _mix_in_kernel(h_ref, w_ref, wg_ref, wal_ref, bal_ref, bg_ref, lng_ref, lnb_ref, ws_ref,
                   bs_ref, wpa_ref, ma_ref, gb_ref, qin_ref, kin_ref, qout_ref, kst_ref,
                   dec_ref, vb_ref, rs_ref):
    tm, d = h_ref.shape
    kd = qin_ref.shape[1]
    h = h_ref[...]

    cols = {}
    start = 0
    for name, width in (("u", lng_ref.shape[1]), ("v", lng_ref.shape[1]), ("qk", 2 * kd),
                        ("vb", vb_ref.shape[1]), ("r", rs_ref.shape[1]), ("alr", LANES)):
        cols[name] = slice(start, start + width)
        start += width

    alr = _dot(h, w_ref[:, cols["alr"]])
    rank_lanes = lax.broadcasted_iota(jnp.int32, alr.shape, 1) < B_GATE_RANK
    alr = jnp.where(rank_lanes, alr, 0.0).astype(BF16)
    vb_ref[...] = _dot(h, w_ref[:, cols["vb"]]).astype(BF16)
    pre = _dot(alr, wal_ref[...]) + bal_ref[...]
    r = _dot(h, w_ref[:, cols["r"]])
    rs_ref[...] = (r * jax.nn.sigmoid(r)).astype(BF16)
    zv = _dot(h, w_ref[:, cols["v"]])
    gates = jax.nn.sigmoid(_dot(h, wg_ref[...]) + bg_ref[...])
    gb_ref[...] = gates[:, d:].astype(BF16)
    qk = _dot(h, w_ref[:, cols["qk"]])
    zu = _dot(h, w_ref[:, cols["u"]])

    nc = tm // B_CHUNK
    bc = _chunk_cumsum(_log2_decay(pre)).reshape(nc, B_CHUNK, kd)
    b_mid = bc[:, B_CHUNK // 2 - 1:B_CHUNK // 2, :]
    b_last = bc[:, B_CHUNK - 1:B_CHUNK, :]
    dec_ref[...] = jnp.exp2(b_last).reshape(nc, kd)

    q = qk[:, :kd].reshape(nc, B_CHUNK, kd) * (B_HEAD_K ** -0.5)
    k = qk[:, kd:].reshape(nc, B_CHUNK, kd)
    qin_ref[...] = (q * jnp.exp2(bc - b_mid)).astype(BF16).reshape(tm, kd)
    kin_ref[...] = (k * jnp.exp2(b_mid - bc)).astype(BF16).reshape(tm, kd)
    qout_ref[...] = (q * jnp.exp2(bc)).astype(BF16).reshape(tm, kd)
    kst_ref[...] = (k * jnp.exp2(b_last - bc)).astype(BF16).reshape(tm, kd)

    v = _gelu(zv)
    mu = jnp.mean(v, axis=-1, keepdims=True)
    vc = v - mu
    var = jnp.mean(vc * vc, axis=-1, keepdims=True)
    vn = (vc * lax.rsqrt(var + EPS) * lng_ref[...] + lnb_ref[...]).astype(BF16)
    width = vn.shape[1]
    gdim = width // A_GROUPS
    sp_rows = []
    for n in range(tm // A_CHUNK):
        blocks = []
        for g in range(A_GROUPS):
            vblk = vn[n * A_CHUNK:(n + 1) * A_CHUNK, g * gdim:(g + 1) * gdim]
            blocks.append(_dot(ws_ref[g], vblk))
        sp_rows.append(jnp.concatenate(blocks, axis=1) + bs_ref[...])
    sp = jnp.concatenate(sp_rows, axis=0)
    ya = _dot((_gelu(zu) * sp).astype(BF16), wpa_ref[...])
    ma_ref[...] = (gates[:, :d] * ya).astype(BF16)


def _mix_in(h, p):
    t, d = h.shape
    tm = MIX_TOKENS

    def tok(width):
        return pl.BlockSpec((tm, width), lambda i: (i, 0))

    weights = [p["w"], p["wg"], p["wal"], p["b_alpha"], p["b_gate"], p["ln_g"], p["ln_b"], p["ws"],
               p["bs"], p["wpa"]]
    kd = p["wal"].shape[1]
    vd = p["wpb"].shape[0]
    out_widths = [d, d, kd, kd, kd, kd, vd, vd]
    out_specs = [tok(w) for w in out_widths]
    out_shape = [jax.ShapeDtypeStruct((t, w), BF16) for w in out_widths]
    out_specs.insert(6, pl.BlockSpec((tm // B_CHUNK, kd), lambda i: (i, 0)))
    out_shape.insert(6, jax.ShapeDtypeStruct((t // B_CHUNK, kd), F32))
    return pl.pallas_call(
        _mix_in_kernel,
        grid=(t // tm,),
        in_specs=[tok(d)] + [_resident(w.shape) for w in weights],
        out_specs=out_specs,
        out_shape=out_shape,
        compiler_params=pltpu.CompilerParams(
            dimension_semantics=("arbitrary",), vmem_limit_bytes=VMEM_LIMIT_BYTES),
        name="mix_in",
    )(h, *weights)


def _gla_kernel(qin_ref, kin_ref, qout_ref, kst_ref, dec_ref, vb_ref, rs_ref, gb_ref,
                ma_ref, x1_ref, hg_ref, wpb_ref, wo_ref, o_ref, st_ref, oacc_ref):
    ts, kd = qin_ref.shape
    nc = ts // B_CHUNK

    @pl.when(pl.program_id(1) == 0)
    def _():
        st_ref[...] = jnp.zeros_like(st_ref)

    def rows(c, n=1):
        return slice(c * B_CHUNK, (c + n) * B_CHUNK)

    def kcols(hd):
        return slice(hd * B_HEAD_K, (hd + 1) * B_HEAD_K)

    def vcols(hd):
        return slice(hd * B_HEAD_V, (hd + 1) * B_HEAD_V)

    pad_rows = jnp.zeros((LANES - nc, B_HEAD_K), F32)
    dec_cols = [jnp.concatenate([dec_ref[:, kcols(hd)], pad_rows], axis=0).T
                for hd in range(B_HEADS)]

    row = lax.broadcasted_iota(jnp.int32, (B_CHUNK, 2 * B_CHUNK), 0)
    col = lax.broadcasted_iota(jnp.int32, (B_CHUNK, 2 * B_CHUNK), 1)
    keep_a = col <= row
    keep_b = (col < B_CHUNK) | (col - B_CHUNK <= row)
    zeros_k = jnp.zeros((B_CHUNK, B_HEAD_K), BF16)
    pairs = [(a, hd) for a in range(0, nc, 2) for hd in range(B_HEADS)]

    scores, kv = {}, {}
    for a, hd in pairs:
        b = a + 1
        kc = kcols(hd)
        kst_a = kst_ref[rows(a), kc]
        s_a = _dot_nt(qin_ref[rows(a), kc],
                      jnp.concatenate([kin_ref[rows(a), kc], zeros_k], axis=0))
        s_b = _dot_nt(jnp.concatenate([qout_ref[rows(b), kc], qin_ref[rows(b), kc]], axis=1),
                      jnp.concatenate([jnp.concatenate([kst_a, zeros_k], axis=1),
                                       jnp.concatenate([zeros_k, kin_ref[rows(b), kc]], axis=1)],
                                      axis=0))
        scores[a, hd] = (jnp.where(keep_a, s_a, 0.0).astype(BF16),
                         jnp.where(keep_b, s_b, 0.0).astype(BF16))
        k_pair = jnp.concatenate([(kst_a.astype(F32) * dec_ref[b:b + 1, kc]).astype(BF16),
                                  kst_ref[rows(b), kc]], axis=0)
        kv[a, hd] = _dot_tn(k_pair, vb_ref[rows(a, 2), vcols(hd)])
    state = [st_ref[hd] for hd in range(B_HEADS)]
    for a, hd in pairs:
        b = a + 1
        kc = kcols(hd)
        q_b = (qout_ref[rows(b), kc].astype(F32) * dec_ref[a:a + 1, kc]).astype(BF16)
        lhs = jnp.concatenate([jnp.concatenate([qout_ref[rows(a), kc], scores[a, hd][0]], axis=1),
                               jnp.concatenate([q_b, scores[a, hd][1]], axis=1)], axis=0)
        rhs = jnp.concatenate([state[hd].astype(BF16), vb_ref[rows(a, 2), vcols(hd)]], axis=0)
        oacc_ref[rows(a, 2), vcols(hd)] = _dot(lhs, rhs)
        dec_ab = dec_cols[hd][:, a:a + 1] * dec_cols[hd][:, b:b + 1]
        state[hd] = dec_ab * state[hd] + kv[a, hd]
    for hd in range(B_HEADS):
        st_ref[hd] = state[hd]

    rs = rs_ref[...].astype(F32)
    parts = []
    for hd in range(B_HEADS):
        parts.append(_rms(oacc_ref[:, vcols(hd)], hg_ref[:, vcols(hd)]) * rs[:, vcols(hd)])
    yb = _dot(jnp.concatenate(parts, axis=1).astype(BF16), wpb_ref[...])
    merged = gb_ref[...].astype(F32) * yb + ma_ref[...].astype(F32)
    o_ref[...] = x1_ref[...] + _dot(merged.astype(BF16), wo_ref[...])


def _gla(qin, kin, qout, kst, dec, vb, rs, gb, ma, x1, p, batch, seq):
    t, d = x1.shape
    ts = GLA_TOKENS
    per_seq = seq // ts
    kd = qin.shape[1]
    vd = vb.shape[1]

    def tok(width, rows=ts):
        return pl.BlockSpec((rows, width), lambda b, s: (b * per_seq + s, 0))

    weights = [p["head_g"], p["wpb"], p["wo"]]
    return pl.pallas_call(
        _gla_kernel,
        grid=(batch, per_seq),
        in_specs=[tok(kd), tok(kd), tok(kd), tok(kd), tok(kd, ts // B_CHUNK), tok(vd),
                  tok(vd), tok(d), tok(d), tok(d)] + [_resident(w.shape) for w in weights],
        out_specs=tok(d),
        out_shape=jax.ShapeDtypeStruct((t, d), F32),
        scratch_shapes=[
            pltpu.VMEM((B_HEADS, B_HEAD_K, B_HEAD_V), F32),
            pltpu.VMEM((ts, vd), F32),
        ],
        compiler_params=pltpu.CompilerParams(
            dimension_semantics=("arbitrary", "arbitrary"),
            vmem_limit_bytes=VMEM_LIMIT_BYTES),
        name="gla",
    )(qin, kin, qout, kst, dec, vb, rs, gb, ma, x1, *weights)


def kernel(x, ffn1_norm, ffn1_w_in, ffn1_w_out, mix_norm, w_in, b_gate, a_ln_gain, a_ln_bias, a_w_s, a_b_s, b_w_alpha, b_b_alpha, b_head_norm, w_proj_a, w_proj_b, w_o, ffn2_norm, ffn2_w_in, ffn2_w_out, final_norm):
    batch, seq, d = x.shape
    a_width = a_ln_gain.shape[1]
    kd = b_w_alpha.shape[2]
    vd = w_proj_b.shape[1]
    assert ffn1_norm.shape[0] == 1, "one layer: the final norm is fused into its last half step"
    assert seq % max(FFN_TOKENS, MIX_TOKENS, GLA_TOKENS) == 0
    assert MIX_TOKENS % A_CHUNK == 0 and GLA_TOKENS % (2 * B_CHUNK) == 0
    xt = x.reshape(batch * seq, d)

    w = w_in[0].astype(BF16)
    o_g = 2 * a_width + 2 * kd + 2 * vd + B_GATE_RANK
    pad = LANES - B_GATE_RANK
    tril = jnp.tril(jnp.ones((A_CHUNK, A_CHUNK), dtype=bool))
    p = {
        "w": w, "wg": w[:, o_g:], "b_gate": b_gate[0].reshape(1, 2 * d),
        "ln_g": a_ln_gain[0].reshape(1, a_width), "ln_b": a_ln_bias[0].reshape(1, a_width),
        "ws": jnp.where(tril[None], a_w_s[0], 0.0).astype(BF16),
        "bs": jnp.repeat(a_b_s[0].T, a_width // A_GROUPS, axis=1),
        "wpa": w_proj_a[0].astype(BF16),
        "wal": jnp.pad(b_w_alpha[0].astype(BF16), ((0, pad), (0, 0))),
        "b_alpha": b_b_alpha[0].reshape(1, kd),
        "head_g": b_head_norm[0].reshape(1, vd),
        "wpb": w_proj_b[0].astype(BF16), "wo": w_o[0].astype(BF16),
    }
    x1, h = _ffn_call(_ffn_mix_kernel, "ffn_mix", xt, ffn1_norm[0], ffn1_w_in[0],
                      ffn1_w_out[0], mix_norm[0], [F32, BF16])
    ma, gb, qin, kin, qout, kst, dec, vb, rs = _mix_in(h, p)
    x2 = _gla(qin, kin, qout, kst, dec, vb, rs, gb, ma, x1, p, batch, seq)
    out, = _ffn_call(_ffn_final_kernel, "ffn_final", x2, ffn2_norm[0], ffn2_w_in[0],
                     ffn2_w_out[0], final_norm, [F32])
    return out.reshape(batch, seq, d)
```

```python
import jax
import jax.numpy as jnp
from jax import lax
from jax.experimental import pallas as pl
from jax.experimental.pallas import tpu as pltpu

F32 = jnp.float32
BF16 = jnp.bfloat16

EPS = 1e-6
LOG2_E = 1.4426950408889634
A_GROUPS = 8
A_CHUNK = 128
B_HEADS = 4
B_HEAD_K = 128
B_HEAD_V = 256
B_GATE_RANK = 16
B_GATE_TAU = 16.0
B_CHUNK = 64
LANES = 128
SUBLANES = 8

FFN_TOKENS = 1024
FFN_SUBBLOCKS = 4
MIX_TOKENS = 512
GLA_TOKENS = 1024
VMEM_LIMIT_BYTES = 56 * 1024 * 1024


def _resident(shape):
    zeros = (0,) * len(shape)
    return pl.BlockSpec(shape, lambda *_: zeros, pipeline_mode=pl.Buffered(1))


def _rms(x, gain):
    return x * lax.rsqrt(jnp.mean(x * x, axis=-1, keepdims=True) + EPS) * gain


def _dot(a, b):
    return jnp.dot(a, b, preferred_element_type=F32)


def _dot_nt(a, b):
    return lax.dot_general(a, b, (((1,), (1,)), ((), ())), preferred_element_type=F32)


def _dot_tn(a, b):
    return lax.dot_general(a, b, (((0,), (0,)), ((), ())), preferred_element_type=F32)


def _swiglu_rows(x, g_ref, wab_ref, wo_ref):
    d_ff = wo_ref.shape[0]
    xn = _rms(x, g_ref[...]).astype(BF16)
    a = _dot(xn, wab_ref[:, :d_ff])
    b = _dot(xn, wab_ref[:, d_ff:])
    hidden = (a * jax.nn.sigmoid(a) * b).astype(BF16)
    return x + 0.5 * _dot(hidden, wo_ref[...])


def _sub_rows(ref, s):
    sub = ref.shape[0] // FFN_SUBBLOCKS
    return slice(s * sub, (s + 1) * sub)


def _ffn_mix_kernel(x_ref, g_ref, wab_ref, wo_ref, mg_ref, x1_ref, h_ref):
    for s in range(FFN_SUBBLOCKS):
        rows = _sub_rows(x_ref, s)
        x1 = _swiglu_rows(x_ref[rows, :], g_ref, wab_ref, wo_ref)
        x1_ref[rows, :] = x1
        h_ref[rows, :] = _rms(x1, mg_ref[...]).astype(BF16)


def _ffn_final_kernel(x1_ref, mix_ref, g_ref, wab_ref, wo_ref, fg_ref, o_ref):
    for s in range(FFN_SUBBLOCKS):
        rows = _sub_rows(x1_ref, s)
        x2 = x1_ref[rows, :] + mix_ref[rows, :]
        y = _swiglu_rows(x2, g_ref, wab_ref, wo_ref)
        o_ref[rows, :] = _rms(y, fg_ref[...])


def _ffn_call(kernel_fn, name, token_inputs, gain, w_in, w_out, out_gain, out_dtypes):
    t, d = token_inputs[0].shape
    d_ff = w_out.shape[0]
    tok = pl.BlockSpec((FFN_TOKENS, d), lambda i: (i, 0))
    return pl.pallas_call(
        kernel_fn,
        grid=(t // FFN_TOKENS,),
        in_specs=[tok] * len(token_inputs)
        + [_resident((1, d)), _resident((d, 2 * d_ff)), _resident((d_ff, d)), _resident((1, d))],
        out_specs=[tok] * len(out_dtypes),
        out_shape=[jax.ShapeDtypeStruct((t, d), dt) for dt in out_dtypes],
        compiler_params=pltpu.CompilerParams(
            dimension_semantics=("arbitrary",), vmem_limit_bytes=VMEM_LIMIT_BYTES),
        name=name,
    )(*token_inputs, gain.reshape(1, d), w_in.astype(BF16), w_out.astype(BF16),
      out_gain.reshape(1, d))


def _gelu(x):
    return 0.5 * x * (1.0 + lax.erf(x * (2.0 ** -0.5)))


def _log2_decay(pre):
    log_sig = jnp.minimum(pre, 0.0) - jnp.log(1.0 + jnp.exp(-jnp.abs(pre)))
    return log_sig * (LOG2_E / B_GATE_TAU)


def _chunk_cumsum(x):
    rows, width = x.shape
    groups = B_CHUNK // SUBLANES
    x = x.reshape(rows // SUBLANES, SUBLANES, width)
    pos = lax.broadcasted_iota(jnp.int32, x.shape, 1)
    shift = 1
    while shift < SUBLANES:
        x = x + jnp.where(pos >= shift, pltpu.roll(x, shift, 1), 0.0)
        shift *= 2
    x = x.reshape(rows // B_CHUNK, groups, SUBLANES, width)
    out = [x[:, 0]]
    for g in range(1, groups):
        total = out[-1][:, SUBLANES - 1:SUBLANES, :]
        out.append(x[:, g] + total)
    return jnp.stack(out, axis=1).reshape(rows, width)


def _mix_in_kernel(h_ref, w_ref, wg_ref, wal_ref, bal_ref, bg_ref, lng_ref, lnb_ref, ws_ref,
                   bs_ref, wpa_ref, ma_ref, gb_ref, qin_ref, kin_ref, qout_ref, kst_ref,
                   dec_ref, vb_ref, rs_ref):
    tm, d = h_ref.shape
    kd = qin_ref.shape[1]
    h = h_ref[...]

    cols = {}
    start = 0
    for name, width in (("u", lng_ref.shape[1]), ("v", lng_ref.shape[1]), ("qk", 2 * kd),
                        ("vb", vb_ref.shape[1]), ("r", rs_ref.shape[1]), ("alr", LANES)):
        cols[name] = slice(start, start + width)
        start += width

    alr = _dot(h, w_ref[:, cols["alr"]])
    rank_lanes = lax.broadcasted_iota(jnp.int32, alr.shape, 1) < B_GATE_RANK
    alr = jnp.where(rank_lanes, alr, 0.0).astype(BF16)
    vb_ref[...] = _dot(h, w_ref[:, cols["vb"]]).astype(BF16)
    pre = _dot(alr, wal_ref[...]) + bal_ref[...]
    r = _dot(h, w_ref[:, cols["r"]])
    rs_ref[...] = (r * jax.nn.sigmoid(r)).astype(BF16)
    zv = _dot(h, w_ref[:, cols["v"]])
    gates = jax.nn.sigmoid(_dot(h, wg_ref[...]) + bg_ref[...])
    gb_ref[...] = gates[:, d:].astype(BF16)
    qk = _dot(h, w_ref[:, cols["qk"]])
    zu = _dot(h, w_ref[:, cols["u"]])

    nc = tm // B_CHUNK
    bc = _chunk_cumsum(_log2_decay(pre)).reshape(nc, B_CHUNK, kd)
    b_mid = bc[:, B_CHUNK // 2 - 1:B_CHUNK // 2, :]
    b_last = bc[:, B_CHUNK - 1:B_CHUNK, :]
    dec_ref[...] = jnp.exp2(b_last).reshape(nc, kd)

    q = qk[:, :kd].reshape(nc, B_CHUNK, kd) * (B_HEAD_K ** -0.5)
    k = qk[:, kd:].reshape(nc, B_CHUNK, kd)
    qin_ref[...] = (q * jnp.exp2(bc - b_mid)).astype(BF16).reshape(tm, kd)
    kin_ref[...] = (k * jnp.exp2(b_mid - bc)).astype(BF16).reshape(tm, kd)
    qout_ref[...] = (q * jnp.exp2(bc)).astype(BF16).reshape(tm, kd)
    kst_ref[...] = (k * jnp.exp2(b_last - bc)).astype(BF16).reshape(tm, kd)

    v = _gelu(zv)
    mu = jnp.mean(v, axis=-1, keepdims=True)
    vc = v - mu
    var = jnp.mean(vc * vc, axis=-1, keepdims=True)
    vn = (vc * lax.rsqrt(var + EPS) * lng_ref[...] + lnb_ref[...]).astype(BF16)
    width = vn.shape[1]
    gdim = width // A_GROUPS
    sp_rows = []
    for n in range(tm // A_CHUNK):
        blocks = []
        for g in range(A_GROUPS):
            vblk = vn[n * A_CHUNK:(n + 1) * A_CHUNK, g * gdim:(g + 1) * gdim]
            blocks.append(_dot(ws_ref[g], vblk))
        sp_rows.append(jnp.concatenate(blocks, axis=1) + bs_ref[...])
    sp = jnp.concatenate(sp_rows, axis=0)
    ya = _dot((_gelu(zu) * sp).astype(BF16), wpa_ref[...])
    ma_ref[...] = (gates[:, :d] * ya).astype(BF16)


def _mix_in(h, p):
    t, d = h.shape
    tm = MIX_TOKENS

    def tok(width):
        return pl.BlockSpec((tm, width), lambda i: (i, 0))

    weights = [p["w"], p["wg"], p["wal"], p["b_alpha"], p["b_gate"], p["ln_g"], p["ln_b"], p["ws"],
               p["bs"], p["wpa"]]
    kd = p["wal"].shape[1]
    vd = p["wpb"].shape[0]
    out_widths = [d, d, kd, kd, kd, kd, vd, vd]
    out_specs = [tok(w) for w in out_widths]
    out_shape = [jax.ShapeDtypeStruct((t, w), BF16) for w in out_widths]
    out_specs.insert(6, pl.BlockSpec((tm // B_CHUNK, kd), lambda i: (i, 0)))
    out_shape.insert(6, jax.ShapeDtypeStruct((t // B_CHUNK, kd), F32))
    return pl.pallas_call(
        _mix_in_kernel,
        grid=(t // tm,),
        in_specs=[tok(d)] + [_resident(w.shape) for w in weights],
        out_specs=out_specs,
        out_shape=out_shape,
        compiler_params=pltpu.CompilerParams(
            dimension_semantics=("arbitrary",), vmem_limit_bytes=VMEM_LIMIT_BYTES),
        name="mix_in",
    )(h, *weights)


def _gla_kernel(qin_ref, kin_ref, qout_ref, kst_ref, dec_ref, vb_ref, rs_ref, gb_ref,
                ma_ref, hg_ref, wpb_ref, wo_ref, o_ref, st_ref, oacc_ref):
    ts, kd = qin_ref.shape
    nc = ts // B_CHUNK

    @pl.when(pl.program_id(1) == 0)
    def _():
        st_ref[...] = jnp.zeros_like(st_ref)

    def rows(c, n=1):
        return slice(c * B_CHUNK, (c + n) * B_CHUNK)

    def kcols(hd):
        return slice(hd * B_HEAD_K, (hd + 1) * B_HEAD_K)

    def vcols(hd):
        return slice(hd * B_HEAD_V, (hd + 1) * B_HEAD_V)

    pad_rows = jnp.zeros((LANES - nc, B_HEAD_K), F32)
    dec_cols = [jnp.concatenate([dec_ref[:, kcols(hd)], pad_rows], axis=0).T
                for hd in range(B_HEADS)]

    row = lax.broadcasted_iota(jnp.int32, (B_CHUNK, 2 * B_CHUNK), 0)
    col = lax.broadcasted_iota(jnp.int32, (B_CHUNK, 2 * B_CHUNK), 1)
    keep_a = col <= row
    keep_b = (col < B_CHUNK) | (col - B_CHUNK <= row)
    zeros_k = jnp.zeros((B_CHUNK, B_HEAD_K), BF16)
    pairs = [(a, hd) for a in range(0, nc, 2) for hd in range(B_HEADS)]

    scores, kv = {}, {}
    for a, hd in pairs:
        b = a + 1
        kc = kcols(hd)
        kst_a = kst_ref[rows(a), kc]
        s_a = _dot_nt(qin_ref[rows(a), kc],
                      jnp.concatenate([kin_ref[rows(a), kc], zeros_k], axis=0))
        s_b = _dot_nt(jnp.concatenate([qout_ref[rows(b), kc], qin_ref[rows(b), kc]], axis=1),
                      jnp.concatenate([jnp.concatenate([kst_a, zeros_k], axis=1),
                                       jnp.concatenate([zeros_k, kin_ref[rows(b), kc]], axis=1)],
                                      axis=0))
        scores[a, hd] = (jnp.where(keep_a, s_a, 0.0).astype(BF16),
                         jnp.where(keep_b, s_b, 0.0).astype(BF16))
        k_pair = jnp.concatenate([(kst_a.astype(F32) * dec_ref[b:b + 1, kc]).astype(BF16),
                                  kst_ref[rows(b), kc]], axis=0)
        kv[a, hd] = _dot_tn(k_pair, vb_ref[rows(a, 2), vcols(hd)])
    state = [st_ref[hd] for hd in range(B_HEADS)]
    for a, hd in pairs:
        b = a + 1
        kc = kcols(hd)
        q_b = (qout_ref[rows(b), kc].astype(F32) * dec_ref[a:a + 1, kc]).astype(BF16)
        lhs = jnp.concatenate([jnp.concatenate([qout_ref[rows(a), kc], scores[a, hd][0]], axis=1),
                               jnp.concatenate([q_b, scores[a, hd][1]], axis=1)], axis=0)
        rhs = jnp.concatenate([state[hd].astype(BF16), vb_ref[rows(a, 2), vcols(hd)]], axis=0)
        oacc_ref[rows(a, 2), vcols(hd)] = _dot(lhs, rhs)
        dec_ab = dec_cols[hd][:, a:a + 1] * dec_cols[hd][:, b:b + 1]
        state[hd] = dec_ab * state[hd] + kv[a, hd]
    for hd in range(B_HEADS):
        st_ref[hd] = state[hd]

    rs = rs_ref[...].astype(F32)
    parts = []
    for hd in range(B_HEADS):
        parts.append(_rms(oacc_ref[:, vcols(hd)], hg_ref[:, vcols(hd)]) * rs[:, vcols(hd)])
    yb = _dot(jnp.concatenate(parts, axis=1).astype(BF16), wpb_ref[...])
    merged = gb_ref[...].astype(F32) * yb + ma_ref[...].astype(F32)
    o_ref[...] = _dot(merged.astype(BF16), wo_ref[...])


def _gla(qin, kin, qout, kst, dec, vb, rs, gb, ma, p, batch, seq):
    t, d = ma.shape
    ts = GLA_TOKENS
    per_seq = seq // ts
    kd = qin.shape[1]
    vd = vb.shape[1]

    def tok(width, rows=ts):
        return pl.BlockSpec((rows, width), lambda b, s: (b * per_seq + s, 0))

    weights = [p["head_g"], p["wpb"], p["wo"]]
    return pl.pallas_call(
        _gla_kernel,
        grid=(batch, per_seq),
        in_specs=[tok(kd), tok(kd), tok(kd), tok(kd), tok(kd, ts // B_CHUNK), tok(vd),
                  tok(vd), tok(d), tok(d)] + [_resident(w.shape) for w in weights],
        out_specs=tok(d),
        out_shape=jax.ShapeDtypeStruct((t, d), F32),
        scratch_shapes=[
            pltpu.VMEM((B_HEADS, B_HEAD_K, B_HEAD_V), F32),
            pltpu.VMEM((ts, vd), F32),
        ],
        compiler_params=pltpu.CompilerParams(
            dimension_semantics=("arbitrary", "arbitrary"),
            vmem_limit_bytes=VMEM_LIMIT_BYTES),
        name="gla",
    )(qin, kin, qout, kst, dec, vb, rs, gb, ma, *weights)


def kernel(x, ffn1_norm, ffn1_w_in, ffn1_w_out, mix_norm, w_in, b_gate, a_ln_gain, a_ln_bias, a_w_s, a_b_s, b_w_alpha, b_b_alpha, b_head_norm, w_proj_a, w_proj_b, w_o, ffn2_norm, ffn2_w_in, ffn2_w_out, final_norm):
    batch, seq, d = x.shape
    a_width = a_ln_gain.shape[1]
    kd = b_w_alpha.shape[2]
    vd = w_proj_b.shape[1]
    assert ffn1_norm.shape[0] == 1, "one layer: the final norm is fused into its last half step"
    assert seq % max(FFN_TOKENS, MIX_TOKENS, GLA_TOKENS) == 0
    assert MIX_TOKENS % A_CHUNK == 0 and GLA_TOKENS % (2 * B_CHUNK) == 0
    xt = x.reshape(batch * seq, d)

    w = w_in[0].astype(BF16)
    o_g = 2 * a_width + 2 * kd + 2 * vd + B_GATE_RANK
    pad = LANES - B_GATE_RANK
    tril = jnp.tril(jnp.ones((A_CHUNK, A_CHUNK), dtype=bool))
    p = {
        "w": w, "wg": w[:, o_g:], "b_gate": b_gate[0].reshape(1, 2 * d),
        "ln_g": a_ln_gain[0].reshape(1, a_width), "ln_b": a_ln_bias[0].reshape(1, a_width),
        "ws": jnp.where(tril[None], a_w_s[0], 0.0).astype(BF16),
        "bs": jnp.repeat(a_b_s[0].T, a_width // A_GROUPS, axis=1),
        "wpa": w_proj_a[0].astype(BF16),
        "wal": jnp.pad(b_w_alpha[0].astype(BF16), ((0, pad), (0, 0))),
        "b_alpha": b_b_alpha[0].reshape(1, kd),
        "head_g": b_head_norm[0].reshape(1, vd),
        "wpb": w_proj_b[0].astype(BF16), "wo": w_o[0].astype(BF16),
    }
    x1, h = _ffn_call(_ffn_mix_kernel, "ffn_mix", [xt], ffn1_norm[0], ffn1_w_in[0],
                      ffn1_w_out[0], mix_norm[0], [F32, BF16])
    ma, gb, qin, kin, qout, kst, dec, vb, rs = _mix_in(h, p)
    mix = _gla(qin, kin, qout, kst, dec, vb, rs, gb, ma, p, batch, seq)
    out, = _ffn_call(_ffn_final_kernel, "ffn_final", [x1, mix], ffn2_norm[0], ffn2_w_in[0],
                     ffn2_w_out[0], final_norm, [F32])
    return out.reshape(batch, seq, d)
```

```python
import jax
import jax.numpy as jnp
from jax import lax
from jax.experimental import pallas as pl
from jax.experimental.pallas import tpu as pltpu

F32 = jnp.float32
BF16 = jnp.bfloat16

EPS = 1e-6
LOG2_E = 1.4426950408889634
A_GROUPS = 8
A_CHUNK = 128
B_HEADS = 4
B_HEAD_K = 128
B_HEAD_V = 256
B_GATE_RANK = 16
B_GATE_TAU = 16.0
B_CHUNK = 64
LANES = 128
SUBLANES = 8

FFN_TOKENS = 1024
FFN_SUBBLOCKS = 4
MIX_TOKENS = 512
GLA_TOKENS = 1024
VMEM_LIMIT_BYTES = 56 * 1024 * 1024


def _resident(shape):
    zeros = (0,) * len(shape)
    return pl.BlockSpec(shape, lambda *_: zeros, pipeline_mode=pl.Buffered(1))


def _rms(x, gain):
    return x * lax.rsqrt(jnp.mean(x * x, axis=-1, keepdims=True) + EPS) * gain


def _dot(a, b):
    return jnp.dot(a, b, preferred_element_type=F32)


def _dot_nt(a, b):
    return lax.dot_general(a, b, (((1,), (1,)), ((), ())), preferred_element_type=F32)


def _dot_tn(a, b):
    return lax.dot_general(a, b, (((0,), (0,)), ((), ())), preferred_element_type=F32)


def _sub_rows(ref, s):
    sub = ref.shape[0] // FFN_SUBBLOCKS
    return slice(s * sub, (s + 1) * sub)


def _swiglu_sub_blocks(load_x, emit, g_ref, wab_ref, wo_ref):
    d_ff = wo_ref.shape[0]

    def begin(s):
        x = load_x(s)
        xn = _rms(x, g_ref[...]).astype(BF16)
        return x, xn, _dot(xn, wab_ref[:, :d_ff])

    nxt = begin(0)
    for s in range(FFN_SUBBLOCKS):
        x, xn, a = nxt
        b = _dot(xn, wab_ref[:, d_ff:])
        if s + 1 < FFN_SUBBLOCKS:
            nxt = begin(s + 1)
        hidden = (a * jax.nn.sigmoid(a) * b).astype(BF16)
        emit(s, x + 0.5 * _dot(hidden, wo_ref[...]))


def _ffn_mix_kernel(x_ref, g_ref, wab_ref, wo_ref, mg_ref, x1_ref, h_ref):
    def emit(s, x1):
        rows = _sub_rows(x_ref, s)
        x1_ref[rows, :] = x1
        h_ref[rows, :] = _rms(x1, mg_ref[...]).astype(BF16)

    _swiglu_sub_blocks(lambda s: x_ref[_sub_rows(x_ref, s), :], emit, g_ref, wab_ref, wo_ref)


def _ffn_final_kernel(x1_ref, mix_ref, g_ref, wab_ref, wo_ref, fg_ref, o_ref):
    def load_x2(s):
        rows = _sub_rows(x1_ref, s)
        return x1_ref[rows, :] + mix_ref[rows, :]

    def emit(s, y):
        o_ref[_sub_rows(o_ref, s), :] = _rms(y, fg_ref[...])

    _swiglu_sub_blocks(load_x2, emit, g_ref, wab_ref, wo_ref)


def _ffn_call(kernel_fn, name, token_inputs, gain, w_in, w_out, out_gain, out_dtypes):
    t, d = token_inputs[0].shape
    d_ff = w_out.shape[0]
    tok = pl.BlockSpec((FFN_TOKENS, d), lambda i: (i, 0))
    return pl.pallas_call(
        kernel_fn,
        grid=(t // FFN_TOKENS,),
        in_specs=[tok] * len(token_inputs)
        + [_resident((1, d)), _resident((d, 2 * d_ff)), _resident((d_ff, d)), _resident((1, d))],
        out_specs=[tok] * len(out_dtypes),
        out_shape=[jax.ShapeDtypeStruct((t, d), dt) for dt in out_dtypes],
        compiler_params=pltpu.CompilerParams(
            dimension_semantics=("arbitrary",), vmem_limit_bytes=VMEM_LIMIT_BYTES),
        name=name,
    )(*token_inputs, gain.reshape(1, d), w_in.astype(BF16), w_out.astype(BF16),
      out_gain.reshape(1, d))


def _gelu(x):
    return 0.5 * x * (1.0 + lax.erf(x * (2.0 ** -0.5)))


def _log2_decay(pre):
    log_sig = jnp.minimum(pre, 0.0) - jnp.log(1.0 + jnp.exp(-jnp.abs(pre)))
    return log_sig * (LOG2_E / B_GATE_TAU)


def _chunk_cumsum(x):
    rows, width = x.shape
    groups = B_CHUNK // SUBLANES
    x = x.reshape(rows // SUBLANES, SUBLANES, width)
    pos = lax.broadcasted_iota(jnp.int32, x.shape, 1)
    shift = 1
    while shift < SUBLANES:
        x = x + jnp.where(pos >= shift, pltpu.roll(x, shift, 1), 0.0)
        shift *= 2
    x = x.reshape(rows // B_CHUNK, groups, SUBLANES, width)
    out = [x[:, 0]]
    for g in range(1, groups):
        total = out[-1][:, SUBLANES - 1:SUBLANES, :]
        out.append(x[:, g] + total)
    return jnp.stack(out, axis=1).reshape(rows, width)


def _mix_in_kernel(h_ref, w_ref, wg_ref, wal_ref, bal_ref, bg_ref, lng_ref, lnb_ref, ws_ref,
                   bs_ref, wpa_ref, ma_ref, gb_ref, qin_ref, kin_ref, qout_ref, kst_ref,
                   dec_ref, vb_ref, rs_ref):
    tm, d = h_ref.shape
    kd = qin_ref.shape[1]
    h = h_ref[...]

    cols = {}
    start = 0
    for name, width in (("u", lng_ref.shape[1]), ("v", lng_ref.shape[1]), ("qk", 2 * kd),
                        ("vb", vb_ref.shape[1]), ("r", rs_ref.shape[1]), ("alr", LANES)):
        cols[name] = slice(start, start + width)
        start += width

    alr = _dot(h, w_ref[:, cols["alr"]])
    rank_lanes = lax.broadcasted_iota(jnp.int32, alr.shape, 1) < B_GATE_RANK
    alr = jnp.where(rank_lanes, alr, 0.0).astype(BF16)
    vb_ref[...] = _dot(h, w_ref[:, cols["vb"]]).astype(BF16)
    pre = _dot(alr, wal_ref[...]) + bal_ref[...]
    r = _dot(h, w_ref[:, cols["r"]])
    rs_ref[...] = (r * jax.nn.sigmoid(r)).astype(BF16)
    zv = _dot(h, w_ref[:, cols["v"]])
    gates = jax.nn.sigmoid(_dot(h, wg_ref[...]) + bg_ref[...])
    gb_ref[...] = gates[:, d:].astype(BF16)
    qk = _dot(h, w_ref[:, cols["qk"]])
    zu = _dot(h, w_ref[:, cols["u"]])

    nc = tm // B_CHUNK
    bc = _chunk_cumsum(_log2_decay(pre)).reshape(nc, B_CHUNK, kd)
    b_mid = bc[:, B_CHUNK // 2 - 1:B_CHUNK // 2, :]
    b_last = bc[:, B_CHUNK - 1:B_CHUNK, :]
    dec_ref[...] = jnp.exp2(b_last).reshape(nc, kd)

    q = qk[:, :kd].reshape(nc, B_CHUNK, kd) * (B_HEAD_K ** -0.5)
    k = qk[:, kd:].reshape(nc, B_CHUNK, kd)
    qin_ref[...] = (q * jnp.exp2(bc - b_mid)).astype(BF16).reshape(tm, kd)
    kin_ref[...] = (k * jnp.exp2(b_mid - bc)).astype(BF16).reshape(tm, kd)
    qout_ref[...] = (q * jnp.exp2(bc)).astype(BF16).reshape(tm, kd)
    kst_ref[...] = (k * jnp.exp2(b_last - bc)).astype(BF16).reshape(tm, kd)

    v = _gelu(zv)
    mu = jnp.mean(v, axis=-1, keepdims=True)
    vc = v - mu
    var = jnp.mean(vc * vc, axis=-1, keepdims=True)
    vn = (vc * lax.rsqrt(var + EPS) * lng_ref[...] + lnb_ref[...]).astype(BF16)
    width = vn.shape[1]
    gdim = width // A_GROUPS
    sp_rows = []
    for n in range(tm // A_CHUNK):
        blocks = []
        for g in range(A_GROUPS):
            vblk = vn[n * A_CHUNK:(n + 1) * A_CHUNK, g * gdim:(g + 1) * gdim]
            blocks.append(_dot(ws_ref[g], vblk))
        sp_rows.append(jnp.concatenate(blocks, axis=1) + bs_ref[...])
    sp = jnp.concatenate(sp_rows, axis=0)
    ya = _dot((_gelu(zu) * sp).astype(BF16), wpa_ref[...])
    ma_ref[...] = (gates[:, :d] * ya).astype(BF16)


def _mix_in(h, p):
    t, d = h.shape
    tm = MIX_TOKENS

    def tok(width):
        return pl.BlockSpec((tm, width), lambda i: (i, 0))

    weights = [p["w"], p["wg"], p["wal"], p["b_alpha"], p["b_gate"], p["ln_g"], p["ln_b"], p["ws"],
               p["bs"], p["wpa"]]
    kd = p["wal"].shape[1]
    vd = p["wpb"].shape[0]
    out_widths = [d, d, kd, kd, kd, kd, vd, vd]
    out_specs = [tok(w) for w in out_widths]
    out_shape = [jax.ShapeDtypeStruct((t, w), BF16) for w in out_widths]
    out_specs.insert(6, pl.BlockSpec((tm // B_CHUNK, kd), lambda i: (i, 0)))
    out_shape.insert(6, jax.ShapeDtypeStruct((t // B_CHUNK, kd), F32))
    return pl.pallas_call(
        _mix_in_kernel,
        grid=(t // tm,),
        in_specs=[tok(d)] + [_resident(w.shape) for w in weights],
        out_specs=out_specs,
        out_shape=out_shape,
        compiler_params=pltpu.CompilerParams(
            dimension_semantics=("arbitrary",), vmem_limit_bytes=VMEM_LIMIT_BYTES),
        name="mix_in",
    )(h, *weights)


def _gla_kernel(qin_ref, kin_ref, qout_ref, kst_ref, dec_ref, vb_ref, rs_ref, gb_ref,
                ma_ref, hg_ref, wpb_ref, wo_ref, o_ref, st_ref, oacc_ref):
    ts, kd = qin_ref.shape
    nc = ts // B_CHUNK

    @pl.when(pl.program_id(1) == 0)
    def _():
        st_ref[...] = jnp.zeros_like(st_ref)

    def rows(c, n=1):
        return slice(c * B_CHUNK, (c + n) * B_CHUNK)

    def kcols(hd):
        return slice(hd * B_HEAD_K, (hd + 1) * B_HEAD_K)

    def vcols(hd):
        return slice(hd * B_HEAD_V, (hd + 1) * B_HEAD_V)

    pad_rows = jnp.zeros((LANES - nc, B_HEAD_K), F32)
    dec_cols = [jnp.concatenate([dec_ref[:, kcols(hd)], pad_rows], axis=0).T
                for hd in range(B_HEADS)]

    row = lax.broadcasted_iota(jnp.int32, (B_CHUNK, 2 * B_CHUNK), 0)
    col = lax.broadcasted_iota(jnp.int32, (B_CHUNK, 2 * B_CHUNK), 1)
    keep_a = col <= row
    keep_b = (col < B_CHUNK) | (col - B_CHUNK <= row)
    zeros_k = jnp.zeros((B_CHUNK, B_HEAD_K), BF16)
    pairs = [(a, hd) for a in range(0, nc, 2) for hd in range(B_HEADS)]

    scores, kv = {}, {}
    for a, hd in pairs:
        b = a + 1
        kc = kcols(hd)
        kst_a = kst_ref[rows(a), kc]
        s_a = _dot_nt(qin_ref[rows(a), kc],
                      jnp.concatenate([kin_ref[rows(a), kc], zeros_k], axis=0))
        s_b = _dot_nt(jnp.concatenate([qout_ref[rows(b), kc], qin_ref[rows(b), kc]], axis=1),
                      jnp.concatenate([jnp.concatenate([kst_a, zeros_k], axis=1),
                                       jnp.concatenate([zeros_k, kin_ref[rows(b), kc]], axis=1)],
                                      axis=0))
        scores[a, hd] = (jnp.where(keep_a, s_a, 0.0).astype(BF16),
                         jnp.where(keep_b, s_b, 0.0).astype(BF16))
        k_pair = jnp.concatenate([(kst_a.astype(F32) * dec_ref[b:b + 1, kc]).astype(BF16),
                                  kst_ref[rows(b), kc]], axis=0)
        kv[a, hd] = _dot_tn(k_pair, vb_ref[rows(a, 2), vcols(hd)])
    state = [st_ref[hd] for hd in range(B_HEADS)]
    for a, hd in pairs:
        b = a + 1
        kc = kcols(hd)
        q_b = (qout_ref[rows(b), kc].astype(F32) * dec_ref[a:a + 1, kc]).astype(BF16)
        lhs = jnp.concatenate([jnp.concatenate([qout_ref[rows(a), kc], scores[a, hd][0]], axis=1),
                               jnp.concatenate([q_b, scores[a, hd][1]], axis=1)], axis=0)
        rhs = jnp.concatenate([state[hd].astype(BF16), vb_ref[rows(a, 2), vcols(hd)]], axis=0)
        oacc_ref[rows(a, 2), vcols(hd)] = _dot(lhs, rhs)
        dec_ab = dec_cols[hd][:, a:a + 1] * dec_cols[hd][:, b:b + 1]
        state[hd] = dec_ab * state[hd] + kv[a, hd]
    for hd in range(B_HEADS):
        st_ref[hd] = state[hd]

    rs = rs_ref[...].astype(F32)
    parts = []
    for hd in range(B_HEADS):
        parts.append(_rms(oacc_ref[:, vcols(hd)], hg_ref[:, vcols(hd)]) * rs[:, vcols(hd)])
    yb = _dot(jnp.concatenate(parts, axis=1).astype(BF16), wpb_ref[...])
    merged = gb_ref[...].astype(F32) * yb + ma_ref[...].astype(F32)
    o_ref[...] = _dot(merged.astype(BF16), wo_ref[...])


def _gla(qin, kin, qout, kst, dec, vb, rs, gb, ma, p, batch, seq):
    t, d = ma.shape
    ts = GLA_TOKENS
    per_seq = seq // ts
    kd = qin.shape[1]
    vd = vb.shape[1]

    def tok(width, rows=ts):
        return pl.BlockSpec((rows, width), lambda b, s: (b * per_seq + s, 0))

    weights = [p["head_g"], p["wpb"], p["wo"]]
    return pl.pallas_call(
        _gla_kernel,
        grid=(batch, per_seq),
        in_specs=[tok(kd), tok(kd), tok(kd), tok(kd), tok(kd, ts // B_CHUNK), tok(vd),
                  tok(vd), tok(d), tok(d)] + [_resident(w.shape) for w in weights],
        out_specs=tok(d),
        out_shape=jax.ShapeDtypeStruct((t, d), F32),
        scratch_shapes=[
            pltpu.VMEM((B_HEADS, B_HEAD_K, B_HEAD_V), F32),
            pltpu.VMEM((ts, vd), F32),
        ],
        compiler_params=pltpu.CompilerParams(
            dimension_semantics=("arbitrary", "arbitrary"),
            vmem_limit_bytes=VMEM_LIMIT_BYTES),
        name="gla",
    )(qin, kin, qout, kst, dec, vb, rs, gb, ma, *weights)


def kernel(x, ffn1_norm, ffn1_w_in, ffn1_w_out, mix_norm, w_in, b_gate, a_ln_gain, a_ln_bias, a_w_s, a_b_s, b_w_alpha, b_b_alpha, b_head_norm, w_proj_a, w_proj_b, w_o, ffn2_norm, ffn2_w_in, ffn2_w_out, final_norm):
    batch, seq, d = x.shape
    a_width = a_ln_gain.shape[1]
    kd = b_w_alpha.shape[2]
    vd = w_proj_b.shape[1]
    assert ffn1_norm.shape[0] == 1, "one layer: the final norm is fused into its last half step"
    assert seq % max(FFN_TOKENS, MIX_TOKENS, GLA_TOKENS) == 0
    assert MIX_TOKENS % A_CHUNK == 0 and GLA_TOKENS % (2 * B_CHUNK) == 0
    xt = x.reshape(batch * seq, d)

    w = w_in[0].astype(BF16)
    o_g = 2 * a_width + 2 * kd + 2 * vd + B_GATE_RANK
    pad = LANES - B_GATE_RANK
    tril = jnp.tril(jnp.ones((A_CHUNK, A_CHUNK), dtype=bool))
    p = {
        "w": w, "wg": w[:, o_g:], "b_gate": b_gate[0].reshape(1, 2 * d),
        "ln_g": a_ln_gain[0].reshape(1, a_width), "ln_b": a_ln_bias[0].reshape(1, a_width),
        "ws": jnp.where(tril[None], a_w_s[0], 0.0).astype(BF16),
        "bs": jnp.repeat(a_b_s[0].T, a_width // A_GROUPS, axis=1),
        "wpa": w_proj_a[0].astype(BF16),
        "wal": jnp.pad(b_w_alpha[0].astype(BF16), ((0, pad), (0, 0))),
        "b_alpha": b_b_alpha[0].reshape(1, kd),
        "head_g": b_head_norm[0].reshape(1, vd),
        "wpb": w_proj_b[0].astype(BF16), "wo": w_o[0].astype(BF16),
    }
    x1, h = _ffn_call(_ffn_mix_kernel, "ffn_mix", [xt], ffn1_norm[0], ffn1_w_in[0],
                      ffn1_w_out[0], mix_norm[0], [F32, BF16])
    ma, gb, qin, kin, qout, kst, dec, vb, rs = _mix_in(h, p)
    mix = _gla(qin, kin, qout, kst, dec, vb, rs, gb, ma, p, batch, seq)
    out, = _ffn_call(_ffn_final_kernel, "ffn_final", [x1, mix], ffn2_norm[0], ffn2_w_in[0],
                     ffn2_w_out[0], final_norm, [F32])
    return out.reshape(batch, seq, d)
```

```python
import jax
import jax.numpy as jnp
from jax import lax
from jax.experimental import pallas as pl
from jax.experimental.pallas import tpu as pltpu

F32 = jnp.float32
BF16 = jnp.bfloat16

EPS = 1e-6
LOG2_E = 1.4426950408889634
A_GROUPS = 8
A_CHUNK = 128
B_HEADS = 4
B_HEAD_K = 128
B_HEAD_V = 256
B_GATE_RANK = 16
B_GATE_TAU = 16.0
B_CHUNK = 64
LANES = 128
SUBLANES = 8

FFN_TOKENS = 1024
FFN_SUBBLOCKS = 4
MIX_TOKENS = 512
GLA_TOKENS = 1024
VMEM_LIMIT_BYTES = 56 * 1024 * 1024


def _resident(shape):
    zeros = (0,) * len(shape)
    return pl.BlockSpec(shape, lambda *_: zeros, pipeline_mode=pl.Buffered(1))


def _rms(x, gain):
    return x * lax.rsqrt(jnp.mean(x * x, axis=-1, keepdims=True) + EPS) * gain


def _dot(a, b):
    return jnp.dot(a, b, preferred_element_type=F32)


def _dot_nt(a, b):
    return lax.dot_general(a, b, (((1,), (1,)), ((), ())), preferred_element_type=F32)


def _dot_tn(a, b):
    return lax.dot_general(a, b, (((0,), (0,)), ((), ())), preferred_element_type=F32)


def _sub_rows(ref, s):
    sub = ref.shape[0] // FFN_SUBBLOCKS
    return slice(s * sub, (s + 1) * sub)


def _swiglu_sub_blocks(load_x, emit, g_ref, wab_ref, wo_ref):
    d_ff = wo_ref.shape[0]

    def begin(s):
        x = load_x(s)
        xn = _rms(x, g_ref[...]).astype(BF16)
        a = _dot(xn, wab_ref[:, :d_ff])
        b = _dot(xn, wab_ref[:, d_ff:])
        return x, (a * jax.nn.sigmoid(a) * b).astype(BF16)

    nxt = begin(0)
    for s in range(FFN_SUBBLOCKS):
        x, hidden = nxt
        if s + 1 < FFN_SUBBLOCKS:
            nxt = begin(s + 1)
        emit(s, x + 0.5 * _dot(hidden, wo_ref[...]))


def _ffn_mix_kernel(x_ref, g_ref, wab_ref, wo_ref, mg_ref, x1_ref, h_ref):
    def emit(s, x1):
        rows = _sub_rows(x_ref, s)
        x1_ref[rows, :] = x1
        h_ref[rows, :] = _rms(x1, mg_ref[...]).astype(BF16)

    _swiglu_sub_blocks(lambda s: x_ref[_sub_rows(x_ref, s), :], emit, g_ref, wab_ref, wo_ref)


def _ffn_final_kernel(x1_ref, mix_ref, g_ref, wab_ref, wo_ref, fg_ref, o_ref):
    def load_x2(s):
        rows = _sub_rows(x1_ref, s)
        return x1_ref[rows, :] + mix_ref[rows, :]

    def emit(s, y):
        o_ref[_sub_rows(o_ref, s), :] = _rms(y, fg_ref[...])

    _swiglu_sub_blocks(load_x2, emit, g_ref, wab_ref, wo_ref)


def _ffn_call(kernel_fn, name, token_inputs, gain, w_in, w_out, out_gain, out_dtypes):
    t, d = token_inputs[0].shape
    d_ff = w_out.shape[0]
    tok = pl.BlockSpec((FFN_TOKENS, d), lambda i: (i, 0))
    return pl.pallas_call(
        kernel_fn,
        grid=(t // FFN_TOKENS,),
        in_specs=[tok] * len(token_inputs)
        + [_resident((1, d)), _resident((d, 2 * d_ff)), _resident((d_ff, d)), _resident((1, d))],
        out_specs=[tok] * len(out_dtypes),
        out_shape=[jax.ShapeDtypeStruct((t, d), dt) for dt in out_dtypes],
        compiler_params=pltpu.CompilerParams(
            dimension_semantics=("arbitrary",), vmem_limit_bytes=VMEM_LIMIT_BYTES),
        name=name,
    )(*token_inputs, gain.reshape(1, d), w_in.astype(BF16), w_out.astype(BF16),
      out_gain.reshape(1, d))


def _gelu(x):
    return 0.5 * x * (1.0 + lax.erf(x * (2.0 ** -0.5)))


def _log2_decay(pre):
    log_sig = jnp.minimum(pre, 0.0) - jnp.log(1.0 + jnp.exp(-jnp.abs(pre)))
    return log_sig * (LOG2_E / B_GATE_TAU)


def _chunk_cumsum(x):
    rows, width = x.shape
    groups = B_CHUNK // SUBLANES
    x = x.reshape(rows // SUBLANES, SUBLANES, width)
    pos = lax.broadcasted_iota(jnp.int32, x.shape, 1)
    shift = 1
    while shift < SUBLANES:
        x = x + jnp.where(pos >= shift, pltpu.roll(x, shift, 1), 0.0)
        shift *= 2
    x = x.reshape(rows // B_CHUNK, groups, SUBLANES, width)
    out = [x[:, 0]]
    for g in range(1, groups):
        total = out[-1][:, SUBLANES - 1:SUBLANES, :]
        out.append(x[:, g] + total)
    return jnp.stack(out, axis=1).reshape(rows, width)


def _mix_in_kernel(h_ref, w_ref, wg_ref, wal_ref, bal_ref, bg_ref, lng_ref, lnb_ref, ws_ref,
                   bs_ref, wpa_ref, ma_ref, gb_ref, qin_ref, kin_ref, qout_ref, kst_ref,
                   dec_ref, vb_ref, rs_ref):
    tm, d = h_ref.shape
    kd = qin_ref.shape[1]
    h = h_ref[...]

    cols = {}
    start = 0
    for name, width in (("u", lng_ref.shape[1]), ("v", lng_ref.shape[1]), ("qk", 2 * kd),
                        ("vb", vb_ref.shape[1]), ("r", rs_ref.shape[1]), ("alr", LANES)):
        cols[name] = slice(start, start + width)
        start += width

    alr = _dot(h, w_ref[:, cols["alr"]])
    rank_lanes = lax.broadcasted_iota(jnp.int32, alr.shape, 1) < B_GATE_RANK
    alr = jnp.where(rank_lanes, alr, 0.0).astype(BF16)
    vb_ref[...] = _dot(h, w_ref[:, cols["vb"]]).astype(BF16)
    pre = _dot(alr, wal_ref[...]) + bal_ref[...]
    r = _dot(h, w_ref[:, cols["r"]])
    rs_ref[...] = (r * jax.nn.sigmoid(r)).astype(BF16)
    zv = _dot(h, w_ref[:, cols["v"]])
    gates = jax.nn.sigmoid(_dot(h, wg_ref[...]) + bg_ref[...])
    gb_ref[...] = gates[:, d:].astype(BF16)
    qk = _dot(h, w_ref[:, cols["qk"]])
    zu = _dot(h, w_ref[:, cols["u"]])

    nc = tm // B_CHUNK
    bc = _chunk_cumsum(_log2_decay(pre)).reshape(nc, B_CHUNK, kd)
    b_mid = bc[:, B_CHUNK // 2 - 1:B_CHUNK // 2, :]
    b_last = bc[:, B_CHUNK - 1:B_CHUNK, :]
    dec_ref[...] = jnp.exp2(b_last).reshape(nc, kd)

    q = qk[:, :kd].reshape(nc, B_CHUNK, kd) * (B_HEAD_K ** -0.5)
    k = qk[:, kd:].reshape(nc, B_CHUNK, kd)
    qin_ref[...] = (q * jnp.exp2(bc - b_mid)).astype(BF16).reshape(tm, kd)
    kin_ref[...] = (k * jnp.exp2(b_mid - bc)).astype(BF16).reshape(tm, kd)
    qout_ref[...] = (q * jnp.exp2(bc)).astype(BF16).reshape(tm, kd)
    kst_ref[...] = (k * jnp.exp2(b_last - bc)).astype(BF16).reshape(tm, kd)

    v = _gelu(zv)
    mu = jnp.mean(v, axis=-1, keepdims=True)
    vc = v - mu
    var = jnp.mean(vc * vc, axis=-1, keepdims=True)
    vn = (vc * lax.rsqrt(var + EPS) * lng_ref[...] + lnb_ref[...]).astype(BF16)
    width = vn.shape[1]
    gdim = width // A_GROUPS
    sp_rows = []
    for n in range(tm // A_CHUNK):
        blocks = []
        for g in range(A_GROUPS):
            vblk = vn[n * A_CHUNK:(n + 1) * A_CHUNK, g * gdim:(g + 1) * gdim]
            blocks.append(_dot(ws_ref[g], vblk))
        sp_rows.append(jnp.concatenate(blocks, axis=1) + bs_ref[...])
    sp = jnp.concatenate(sp_rows, axis=0)
    ya = _dot((_gelu(zu) * sp).astype(BF16), wpa_ref[...])
    ma_ref[...] = (gates[:, :d] * ya).astype(BF16)


def _mix_in(h, p):
    t, d = h.shape
    tm = MIX_TOKENS

    def tok(width):
        return pl.BlockSpec((tm, width), lambda i: (i, 0))

    weights = [p["w"], p["wg"], p["wal"], p["b_alpha"], p["b_gate"], p["ln_g"], p["ln_b"], p["ws"],
               p["bs"], p["wpa"]]
    kd = p["wal"].shape[1]
    vd = p["wpb"].shape[0]
    out_widths = [d, d, kd, kd, kd, kd, vd, vd]
    out_specs = [tok(w) for w in out_widths]
    out_shape = [jax.ShapeDtypeStruct((t, w), BF16) for w in out_widths]
    out_specs.insert(6, pl.BlockSpec((tm // B_CHUNK, kd), lambda i: (i, 0)))
    out_shape.insert(6, jax.ShapeDtypeStruct((t // B_CHUNK, kd), F32))
    return pl.pallas_call(
        _mix_in_kernel,
        grid=(t // tm,),
        in_specs=[tok(d)] + [_resident(w.shape) for w in weights],
        out_specs=out_specs,
        out_shape=out_shape,
        compiler_params=pltpu.CompilerParams(
            dimension_semantics=("arbitrary",), vmem_limit_bytes=VMEM_LIMIT_BYTES),
        name="mix_in",
    )(h, *weights)


def _gla_kernel(qin_ref, kin_ref, qout_ref, kst_ref, dec_ref, vb_ref, rs_ref, gb_ref,
                ma_ref, hg_ref, wpb_ref, wo_ref, o_ref, st_ref, oacc_ref):
    ts, kd = qin_ref.shape
    nc = ts // B_CHUNK

    @pl.when(pl.program_id(1) == 0)
    def _():
        st_ref[...] = jnp.zeros_like(st_ref)

    def rows(c, n=1):
        return slice(c * B_CHUNK, (c + n) * B_CHUNK)

    def kcols(hd):
        return slice(hd * B_HEAD_K, (hd + 1) * B_HEAD_K)

    def vcols(hd):
        return slice(hd * B_HEAD_V, (hd + 1) * B_HEAD_V)

    pad_rows = jnp.zeros((LANES - nc, B_HEAD_K), F32)
    dec_cols = [jnp.concatenate([dec_ref[:, kcols(hd)], pad_rows], axis=0).T
                for hd in range(B_HEADS)]

    row = lax.broadcasted_iota(jnp.int32, (B_CHUNK, 2 * B_CHUNK), 0)
    col = lax.broadcasted_iota(jnp.int32, (B_CHUNK, 2 * B_CHUNK), 1)
    keep_a = col <= row
    keep_b = (col < B_CHUNK) | (col - B_CHUNK <= row)
    zeros_k = jnp.zeros((B_CHUNK, B_HEAD_K), BF16)
    pairs = [(a, hd) for a in range(0, nc, 2) for hd in range(B_HEADS)]

    scores, kv = {}, {}
    for a, hd in pairs:
        b = a + 1
        kc = kcols(hd)
        kst_a = kst_ref[rows(a), kc]
        s_a = _dot_nt(qin_ref[rows(a), kc],
                      jnp.concatenate([kin_ref[rows(a), kc], zeros_k], axis=0))
        s_b = _dot_nt(jnp.concatenate([qout_ref[rows(b), kc], qin_ref[rows(b), kc]], axis=1),
                      jnp.concatenate([jnp.concatenate([kst_a, zeros_k], axis=1),
                                       jnp.concatenate([zeros_k, kin_ref[rows(b), kc]], axis=1)],
                                      axis=0))
        scores[a, hd] = (jnp.where(keep_a, s_a, 0.0).astype(BF16),
                         jnp.where(keep_b, s_b, 0.0).astype(BF16))
        k_pair = jnp.concatenate([(kst_a.astype(F32) * dec_ref[b:b + 1, kc]).astype(BF16),
                                  kst_ref[rows(b), kc]], axis=0)
        kv[a, hd] = _dot_tn(k_pair, vb_ref[rows(a, 2), vcols(hd)])
    state = [st_ref[hd] for hd in range(B_HEADS)]
    for a, hd in pairs:
        b = a + 1
        kc = kcols(hd)
        q_b = (qout_ref[rows(b), kc].astype(F32) * dec_ref[a:a + 1, kc]).astype(BF16)
        lhs = jnp.concatenate([jnp.concatenate([qout_ref[rows(a), kc], scores[a, hd][0]], axis=1),
                               jnp.concatenate([q_b, scores[a, hd][1]], axis=1)], axis=0)
        rhs = jnp.concatenate([state[hd].astype(BF16), vb_ref[rows(a, 2), vcols(hd)]], axis=0)
        oacc_ref[rows(a, 2), vcols(hd)] = _dot(lhs, rhs)
        dec_ab = dec_cols[hd][:, a:a + 1] * dec_cols[hd][:, b:b + 1]
        state[hd] = dec_ab * state[hd] + kv[a, hd]
    for hd in range(B_HEADS):
        st_ref[hd] = state[hd]

    rs = rs_ref[...].astype(F32)
    parts = []
    for hd in range(B_HEADS):
        parts.append(_rms(oacc_ref[:, vcols(hd)], hg_ref[:, vcols(hd)]) * rs[:, vcols(hd)])
    yb = _dot(jnp.concatenate(parts, axis=1).astype(BF16), wpb_ref[...])
    merged = gb_ref[...].astype(F32) * yb + ma_ref[...].astype(F32)
    o_ref[...] = _dot(merged.astype(BF16), wo_ref[...])


def _gla(qin, kin, qout, kst, dec, vb, rs, gb, ma, p, batch, seq):
    t, d = ma.shape
    ts = GLA_TOKENS
    per_seq = seq // ts
    kd = qin.shape[1]
    vd = vb.shape[1]

    def tok(width, rows=ts):
        return pl.BlockSpec((rows, width), lambda b, s: (b * per_seq + s, 0))

    weights = [p["head_g"], p["wpb"], p["wo"]]
    return pl.pallas_call(
        _gla_kernel,
        grid=(batch, per_seq),
        in_specs=[tok(kd), tok(kd), tok(kd), tok(kd), tok(kd, ts // B_CHUNK), tok(vd),
                  tok(vd), tok(d), tok(d)] + [_resident(w.shape) for w in weights],
        out_specs=tok(d),
        out_shape=jax.ShapeDtypeStruct((t, d), F32),
        scratch_shapes=[
            pltpu.VMEM((B_HEADS, B_HEAD_K, B_HEAD_V), F32),
            pltpu.VMEM((ts, vd), F32),
        ],
        compiler_params=pltpu.CompilerParams(
            dimension_semantics=("arbitrary", "arbitrary"),
            vmem_limit_bytes=VMEM_LIMIT_BYTES),
        name="gla",
    )(qin, kin, qout, kst, dec, vb, rs, gb, ma, *weights)


def kernel(x, ffn1_norm, ffn1_w_in, ffn1_w_out, mix_norm, w_in, b_gate, a_ln_gain, a_ln_bias, a_w_s, a_b_s, b_w_alpha, b_b_alpha, b_head_norm, w_proj_a, w_proj_b, w_o, ffn2_norm, ffn2_w_in, ffn2_w_out, final_norm):
    batch, seq, d = x.shape
    a_width = a_ln_gain.shape[1]
    kd = b_w_alpha.shape[2]
    vd = w_proj_b.shape[1]
    assert ffn1_norm.shape[0] == 1, "one layer: the final norm is fused into its last half step"
    assert seq % max(FFN_TOKENS, MIX_TOKENS, GLA_TOKENS) == 0
    assert MIX_TOKENS % A_CHUNK == 0 and GLA_TOKENS % (2 * B_CHUNK) == 0
    xt = x.reshape(batch * seq, d)

    w = w_in[0].astype(BF16)
    o_g = 2 * a_width + 2 * kd + 2 * vd + B_GATE_RANK
    pad = LANES - B_GATE_RANK
    tril = jnp.tril(jnp.ones((A_CHUNK, A_CHUNK), dtype=bool))
    p = {
        "w": w, "wg": w[:, o_g:], "b_gate": b_gate[0].reshape(1, 2 * d),
        "ln_g": a_ln_gain[0].reshape(1, a_width), "ln_b": a_ln_bias[0].reshape(1, a_width),
        "ws": jnp.where(tril[None], a_w_s[0], 0.0).astype(BF16),
        "bs": jnp.repeat(a_b_s[0].T, a_width // A_GROUPS, axis=1),
        "wpa": w_proj_a[0].astype(BF16),
        "wal": jnp.pad(b_w_alpha[0].astype(BF16), ((0, pad), (0, 0))),
        "b_alpha": b_b_alpha[0].reshape(1, kd),
        "head_g": b_head_norm[0].reshape(1, vd),
        "wpb": w_proj_b[0].astype(BF16), "wo": w_o[0].astype(BF16),
    }
    x1, h = _ffn_call(_ffn_mix_kernel, "ffn_mix", [xt], ffn1_norm[0], ffn1_w_in[0],
                      ffn1_w_out[0], mix_norm[0], [F32, BF16])
    ma, gb, qin, kin, qout, kst, dec, vb, rs = _mix_in(h, p)
    mix = _gla(qin, kin, qout, kst, dec, vb, rs, gb, ma, p, batch, seq)
    out, = _ffn_call(_ffn_final_kernel, "ffn_final", [x1, mix], ffn2_norm[0], ffn2_w_in[0],
                     ffn2_w_out[0], final_norm, [F32])
    return out.reshape(batch, seq, d)
```

```python
import jax
import jax.numpy as jnp
from jax import lax
from jax.experimental import pallas as pl
from jax.experimental.pallas import tpu as pltpu

F32 = jnp.float32
BF16 = jnp.bfloat16

EPS = 1e-6
LOG2_E = 1.4426950408889634
A_GROUPS = 8
A_CHUNK = 128
B_HEADS = 4
B_HEAD_K = 128
B_HEAD_V = 256
B_GATE_RANK = 16
B_GATE_TAU = 16.0
B_CHUNK = 64
LANES = 128
SUBLANES = 8

FFN_TOKENS = 1024
FFN_SUBBLOCKS = 4
MIX_TOKENS = 512
GLA_TOKENS = 1024
VMEM_LIMIT_BYTES = 56 * 1024 * 1024


def _resident(shape):
    zeros = (0,) * len(shape)
    return pl.BlockSpec(shape, lambda *_: zeros, pipeline_mode=pl.Buffered(1))


def _rms(x, gain):
    return x * lax.rsqrt(jnp.mean(x * x, axis=-1, keepdims=True) + EPS) * gain


def _dot(a, b):
    return jnp.dot(a, b, preferred_element_type=F32)


def _dot_nt(a, b):
    return lax.dot_general(a, b, (((1,), (1,)), ((), ())), preferred_element_type=F32)


def _dot_tn(a, b):
    return lax.dot_general(a, b, (((0,), (0,)), ((), ())), preferred_element_type=F32)


def _sub_rows(ref, s):
    sub = ref.shape[0] // FFN_SUBBLOCKS
    return slice(s * sub, (s + 1) * sub)


def _swiglu_sub_blocks(load_x, emit, g_ref, wab_ref, wo_ref):
    d_ff = wo_ref.shape[0]

    def begin(s):
        x = load_x(s)
        xn = _rms(x, g_ref[...]).astype(BF16)
        a = _dot(xn, wab_ref[:, :d_ff])
        b = _dot(xn, wab_ref[:, d_ff:])
        return x, (a * jax.nn.sigmoid(a) * b).astype(BF16)

    ahead = [begin(0), begin(1)]
    for s in range(FFN_SUBBLOCKS):
        x, hidden = ahead.pop(0)
        if s + 2 < FFN_SUBBLOCKS:
            ahead.append(begin(s + 2))
        emit(s, x + 0.5 * _dot(hidden, wo_ref[...]))


def _ffn_mix_kernel(x_ref, g_ref, wab_ref, wo_ref, mg_ref, x1_ref, h_ref):
    def emit(s, x1):
        rows = _sub_rows(x_ref, s)
        x1_ref[rows, :] = x1
        h_ref[rows, :] = _rms(x1, mg_ref[...]).astype(BF16)

    _swiglu_sub_blocks(lambda s: x_ref[_sub_rows(x_ref, s), :], emit, g_ref, wab_ref, wo_ref)


def _ffn_final_kernel(x1_ref, mix_ref, g_ref, wab_ref, wo_ref, fg_ref, o_ref):
    def load_x2(s):
        rows = _sub_rows(x1_ref, s)
        return x1_ref[rows, :] + mix_ref[rows, :]

    def emit(s, y):
        o_ref[_sub_rows(o_ref, s), :] = _rms(y, fg_ref[...])

    _swiglu_sub_blocks(load_x2, emit, g_ref, wab_ref, wo_ref)


def _ffn_call(kernel_fn, name, token_inputs, gain, w_in, w_out, out_gain, out_dtypes):
    t, d = token_inputs[0].shape
    d_ff = w_out.shape[0]
    tok = pl.BlockSpec((FFN_TOKENS, d), lambda i: (i, 0))
    return pl.pallas_call(
        kernel_fn,
        grid=(t // FFN_TOKENS,),
        in_specs=[tok] * len(token_inputs)
        + [_resident((1, d)), _resident((d, 2 * d_ff)), _resident((d_ff, d)), _resident((1, d))],
        out_specs=[tok] * len(out_dtypes),
        out_shape=[jax.ShapeDtypeStruct((t, d), dt) for dt in out_dtypes],
        compiler_params=pltpu.CompilerParams(
            dimension_semantics=("arbitrary",), vmem_limit_bytes=VMEM_LIMIT_BYTES),
        name=name,
    )(*token_inputs, gain.reshape(1, d), w_in.astype(BF16), w_out.astype(BF16),
      out_gain.reshape(1, d))


def _gelu(x):
    return 0.5 * x * (1.0 + lax.erf(x * (2.0 ** -0.5)))


def _log2_decay(pre):
    log_sig = jnp.minimum(pre, 0.0) - jnp.log(1.0 + jnp.exp(-jnp.abs(pre)))
    return log_sig * (LOG2_E / B_GATE_TAU)


def _chunk_cumsum(x):
    rows, width = x.shape
    groups = B_CHUNK // SUBLANES
    x = x.reshape(rows // SUBLANES, SUBLANES, width)
    pos = lax.broadcasted_iota(jnp.int32, x.shape, 1)
    shift = 1
    while shift < SUBLANES:
        x = x + jnp.where(pos >= shift, pltpu.roll(x, shift, 1), 0.0)
        shift *= 2
    x = x.reshape(rows // B_CHUNK, groups, SUBLANES, width)
    out = [x[:, 0]]
    for g in range(1, groups):
        total = out[-1][:, SUBLANES - 1:SUBLANES, :]
        out.append(x[:, g] + total)
    return jnp.stack(out, axis=1).reshape(rows, width)


def _mix_in_kernel(h_ref, w_ref, wg_ref, wal_ref, bal_ref, bg_ref, lng_ref, lnb_ref, ws_ref,
                   bs_ref, wpa_ref, ma_ref, gb_ref, qin_ref, kin_ref, qout_ref, kst_ref,
                   dec_ref, vb_ref, rs_ref):
    tm, d = h_ref.shape
    kd = qin_ref.shape[1]
    h = h_ref[...]

    cols = {}
    start = 0
    for name, width in (("u", lng_ref.shape[1]), ("v", lng_ref.shape[1]), ("qk", 2 * kd),
                        ("vb", vb_ref.shape[1]), ("r", rs_ref.shape[1]), ("alr", LANES)):
        cols[name] = slice(start, start + width)
        start += width

    alr = _dot(h, w_ref[:, cols["alr"]])
    rank_lanes = lax.broadcasted_iota(jnp.int32, alr.shape, 1) < B_GATE_RANK
    alr = jnp.where(rank_lanes, alr, 0.0).astype(BF16)
    vb_ref[...] = _dot(h, w_ref[:, cols["vb"]]).astype(BF16)
    pre = _dot(alr, wal_ref[...]) + bal_ref[...]
    r = _dot(h, w_ref[:, cols["r"]])
    rs_ref[...] = (r * jax.nn.sigmoid(r)).astype(BF16)
    zv = _dot(h, w_ref[:, cols["v"]])
    gates = jax.nn.sigmoid(_dot(h, wg_ref[...]) + bg_ref[...])
    gb_ref[...] = gates[:, d:].astype(BF16)
    qk = _dot(h, w_ref[:, cols["qk"]])
    zu = _dot(h, w_ref[:, cols["u"]])

    nc = tm // B_CHUNK
    bc = _chunk_cumsum(_log2_decay(pre)).reshape(nc, B_CHUNK, kd)
    b_mid = bc[:, B_CHUNK // 2 - 1:B_CHUNK // 2, :]
    b_last = bc[:, B_CHUNK - 1:B_CHUNK, :]
    dec_ref[...] = jnp.exp2(b_last).reshape(nc, kd)

    q = qk[:, :kd].reshape(nc, B_CHUNK, kd) * (B_HEAD_K ** -0.5)
    k = qk[:, kd:].reshape(nc, B_CHUNK, kd)
    qin_ref[...] = (q * jnp.exp2(bc - b_mid)).astype(BF16).reshape(tm, kd)
    kin_ref[...] = (k * jnp.exp2(b_mid - bc)).astype(BF16).reshape(tm, kd)
    qout_ref[...] = (q * jnp.exp2(bc)).astype(BF16).reshape(tm, kd)
    kst_ref[...] = (k * jnp.exp2(b_last - bc)).astype(BF16).reshape(tm, kd)

    v = _gelu(zv)
    mu = jnp.mean(v, axis=-1, keepdims=True)
    vc = v - mu
    var = jnp.mean(vc * vc, axis=-1, keepdims=True)
    vn = (vc * lax.rsqrt(var + EPS) * lng_ref[...] + lnb_ref[...]).astype(BF16)
    width = vn.shape[1]
    gdim = width // A_GROUPS
    sp_rows = []
    for n in range(tm // A_CHUNK):
        blocks = []
        for g in range(A_GROUPS):
            vblk = vn[n * A_CHUNK:(n + 1) * A_CHUNK, g * gdim:(g + 1) * gdim]
            blocks.append(_dot(ws_ref[g], vblk))
        sp_rows.append(jnp.concatenate(blocks, axis=1) + bs_ref[...])
    sp = jnp.concatenate(sp_rows, axis=0)
    ya = _dot((_gelu(zu) * sp).astype(BF16), wpa_ref[...])
    ma_ref[...] = (gates[:, :d] * ya).astype(BF16)


def _mix_in(h, p):
    t, d = h.shape
    tm = MIX_TOKENS

    def tok(width):
        return pl.BlockSpec((tm, width), lambda i: (i, 0))

    weights = [p["w"], p["wg"], p["wal"], p["b_alpha"], p["b_gate"], p["ln_g"], p["ln_b"], p["ws"],
               p["bs"], p["wpa"]]
    kd = p["wal"].shape[1]
    vd = p["wpb"].shape[0]
    out_widths = [d, d, kd, kd, kd, kd, vd, vd]
    out_specs = [tok(w) for w in out_widths]
    out_shape = [jax.ShapeDtypeStruct((t, w), BF16) for w in out_widths]
    out_specs.insert(6, pl.BlockSpec((tm // B_CHUNK, kd), lambda i: (i, 0)))
    out_shape.insert(6, jax.ShapeDtypeStruct((t // B_CHUNK, kd), F32))
    return pl.pallas_call(
        _mix_in_kernel,
        grid=(t // tm,),
        in_specs=[tok(d)] + [_resident(w.shape) for w in weights],
        out_specs=out_specs,
        out_shape=out_shape,
        compiler_params=pltpu.CompilerParams(
            dimension_semantics=("arbitrary",), vmem_limit_bytes=VMEM_LIMIT_BYTES),
        name="mix_in",
    )(h, *weights)


def _gla_kernel(qin_ref, kin_ref, qout_ref, kst_ref, dec_ref, vb_ref, rs_ref, gb_ref,
                ma_ref, hg_ref, wpb_ref, wo_ref, o_ref, st_ref, oacc_ref):
    ts, kd = qin_ref.shape
    nc = ts // B_CHUNK

    @pl.when(pl.program_id(1) == 0)
    def _():
        st_ref[...] = jnp.zeros_like(st_ref)

    def rows(c, n=1):
        return slice(c * B_CHUNK, (c + n) * B_CHUNK)

    def kcols(hd):
        return slice(hd * B_HEAD_K, (hd + 1) * B_HEAD_K)

    def vcols(hd):
        return slice(hd * B_HEAD_V, (hd + 1) * B_HEAD_V)

    pad_rows = jnp.zeros((LANES - nc, B_HEAD_K), F32)
    dec_cols = [jnp.concatenate([dec_ref[:, kcols(hd)], pad_rows], axis=0).T
                for hd in range(B_HEADS)]

    row = lax.broadcasted_iota(jnp.int32, (B_CHUNK, 2 * B_CHUNK), 0)
    col = lax.broadcasted_iota(jnp.int32, (B_CHUNK, 2 * B_CHUNK), 1)
    keep_a = col <= row
    keep_b = (col < B_CHUNK) | (col - B_CHUNK <= row)
    zeros_k = jnp.zeros((B_CHUNK, B_HEAD_K), BF16)
    pairs = [(a, hd) for a in range(0, nc, 2) for hd in range(B_HEADS)]

    scores, kv = {}, {}
    for a, hd in pairs:
        b = a + 1
        kc = kcols(hd)
        kst_a = kst_ref[rows(a), kc]
        s_a = _dot_nt(qin_ref[rows(a), kc],
                      jnp.concatenate([kin_ref[rows(a), kc], zeros_k], axis=0))
        s_b = _dot_nt(jnp.concatenate([qout_ref[rows(b), kc], qin_ref[rows(b), kc]], axis=1),
                      jnp.concatenate([jnp.concatenate([kst_a, zeros_k], axis=1),
                                       jnp.concatenate([zeros_k, kin_ref[rows(b), kc]], axis=1)],
                                      axis=0))
        scores[a, hd] = (jnp.where(keep_a, s_a, 0.0).astype(BF16),
                         jnp.where(keep_b, s_b, 0.0).astype(BF16))
        k_pair = jnp.concatenate([(kst_a.astype(F32) * dec_ref[b:b + 1, kc]).astype(BF16),
                                  kst_ref[rows(b), kc]], axis=0)
        kv[a, hd] = _dot_tn(k_pair, vb_ref[rows(a, 2), vcols(hd)])
    state = [st_ref[hd] for hd in range(B_HEADS)]
    for a, hd in pairs:
        b = a + 1
        kc = kcols(hd)
        q_b = (qout_ref[rows(b), kc].astype(F32) * dec_ref[a:a + 1, kc]).astype(BF16)
        lhs = jnp.concatenate([jnp.concatenate([qout_ref[rows(a), kc], scores[a, hd][0]], axis=1),
                               jnp.concatenate([q_b, scores[a, hd][1]], axis=1)], axis=0)
        rhs = jnp.concatenate([state[hd].astype(BF16), vb_ref[rows(a, 2), vcols(hd)]], axis=0)
        oacc_ref[rows(a, 2), vcols(hd)] = _dot(lhs, rhs)
        dec_ab = dec_cols[hd][:, a:a + 1] * dec_cols[hd][:, b:b + 1]
        state[hd] = dec_ab * state[hd] + kv[a, hd]
    for hd in range(B_HEADS):
        st_ref[hd] = state[hd]

    rs = rs_ref[...].astype(F32)
    parts = []
    for hd in range(B_HEADS):
        parts.append(_rms(oacc_ref[:, vcols(hd)], hg_ref[:, vcols(hd)]) * rs[:, vcols(hd)])
    yb = _dot(jnp.concatenate(parts, axis=1).astype(BF16), wpb_ref[...])
    merged = gb_ref[...].astype(F32) * yb + ma_ref[...].astype(F32)
    o_ref[...] = _dot(merged.astype(BF16), wo_ref[...])


def _gla(qin, kin, qout, kst, dec, vb, rs, gb, ma, p, batch, seq):
    t, d = ma.shape
    ts = GLA_TOKENS
    per_seq = seq // ts
    kd = qin.shape[1]
    vd = vb.shape[1]

    def tok(width, rows=ts):
        return pl.BlockSpec((rows, width), lambda b, s: (b * per_seq + s, 0))

    weights = [p["head_g"], p["wpb"], p["wo"]]
    return pl.pallas_call(
        _gla_kernel,
        grid=(batch, per_seq),
        in_specs=[tok(kd), tok(kd), tok(kd), tok(kd), tok(kd, ts // B_CHUNK), tok(vd),
                  tok(vd), tok(d), tok(d)] + [_resident(w.shape) for w in weights],
        out_specs=tok(d),
        out_shape=jax.ShapeDtypeStruct((t, d), F32),
        scratch_shapes=[
            pltpu.VMEM((B_HEADS, B_HEAD_K, B_HEAD_V), F32),
            pltpu.VMEM((ts, vd), F32),
        ],
        compiler_params=pltpu.CompilerParams(
            dimension_semantics=("arbitrary", "arbitrary"),
            vmem_limit_bytes=VMEM_LIMIT_BYTES),
        name="gla",
    )(qin, kin, qout, kst, dec, vb, rs, gb, ma, *weights)


def kernel(x, ffn1_norm, ffn1_w_in, ffn1_w_out, mix_norm, w_in, b_gate, a_ln_gain, a_ln_bias, a_w_s, a_b_s, b_w_alpha, b_b_alpha, b_head_norm, w_proj_a, w_proj_b, w_o, ffn2_norm, ffn2_w_in, ffn2_w_out, final_norm):
    batch, seq, d = x.shape
    a_width = a_ln_gain.shape[1]
    kd = b_w_alpha.shape[2]
    vd = w_proj_b.shape[1]
    assert ffn1_norm.shape[0] == 1, "one layer: the final norm is fused into its last half step"
    assert seq % max(FFN_TOKENS, MIX_TOKENS, GLA_TOKENS) == 0
    assert MIX_TOKENS % A_CHUNK == 0 and GLA_TOKENS % (2 * B_CHUNK) == 0
    xt = x.reshape(batch * seq, d)

    w = w_in[0].astype(BF16)
    o_g = 2 * a_width + 2 * kd + 2 * vd + B_GATE_RANK
    pad = LANES - B_GATE_RANK
    tril = jnp.tril(jnp.ones((A_CHUNK, A_CHUNK), dtype=bool))
    p = {
        "w": w, "wg": w[:, o_g:], "b_gate": b_gate[0].reshape(1, 2 * d),
        "ln_g": a_ln_gain[0].reshape(1, a_width), "ln_b": a_ln_bias[0].reshape(1, a_width),
        "ws": jnp.where(tril[None], a_w_s[0], 0.0).astype(BF16),
        "bs": jnp.repeat(a_b_s[0].T, a_width // A_GROUPS, axis=1),
        "wpa": w_proj_a[0].astype(BF16),
        "wal": jnp.pad(b_w_alpha[0].astype(BF16), ((0, pad), (0, 0))),
        "b_alpha": b_b_alpha[0].reshape(1, kd),
        "head_g": b_head_norm[0].reshape(1, vd),
        "wpb": w_proj_b[0].astype(BF16), "wo": w_o[0].astype(BF16),
    }
    x1, h = _ffn_call(_ffn_mix_kernel, "ffn_mix", [xt], ffn1_norm[0], ffn1_w_in[0],
                      ffn1_w_out[0], mix_norm[0], [F32, BF16])
    ma, gb, qin, kin, qout, kst, dec, vb, rs = _mix_in(h, p)
    mix = _gla(qin, kin, qout, kst, dec, vb, rs, gb, ma, p, batch, seq)
    out, = _ffn_call(_ffn_final_kernel, "ffn_final", [x1, mix], ffn2_norm[0], ffn2_w_in[0],
                     ffn2_w_out[0], final_norm, [F32])
    return out.reshape(batch, seq, d)
```
